```python
import math
import jax
import jax.numpy as jnp
from jax import lax
import numpy as np

D_MODEL = 1024
BATCH = 8
SEQ = 8192
DEPTH = 1
DEC_BATCH = 16
DEC_SEQ = 16
PAST_LEN = 2048

CHUNK = 64
Q_BLOCK = 128
N_HEADS_A = 4
HEAD_DIM_A = 64
WIDTH_A = N_HEADS_A * 2 * HEAD_DIM_A
N_HEADS_B = 8
HEAD_DIM_B = 64
WIDTH_B = N_HEADS_B * HEAD_DIM_B
IN_WIDTH = 4 * WIDTH_A + 4 * WIDTH_B + 2 * D_MODEL
ROPE_THETA = 10000.0
NORM_EPS = 1e-6

kernel_name = 'streaming_diff_stickbreak_hybrid'


def rms_norm(x, g):
    xf = x.astype(jnp.float32)
    y = xf * lax.rsqrt(jnp.mean(xf * xf, axis=-1, keepdims=True) + NORM_EPS)
    return (y * g.astype(jnp.float32)).astype(x.dtype)


def rope(x, pos):
    half = x.shape[-1] // 2
    inv = ROPE_THETA ** (-jnp.arange(half, dtype=jnp.float32) / half)
    ang = pos.astype(jnp.float32)[:, None] * inv[None, :]
    cos = jnp.cos(ang)[None, :, None, None, :]
    sin = jnp.sin(ang)[None, :, None, None, :]
    xf = x.astype(jnp.float32)
    x1, x2 = xf[..., :half], xf[..., half:]
    return jnp.concatenate([x1 * cos - x2 * sin, x2 * cos + x1 * sin], axis=-1).astype(x.dtype)


def adaln_modulate(x, c, norm_g, w_ada, b_ada):
    mod = c @ w_ada + b_ada
    shift, scale, gate = jnp.split(mod, 3, axis=-1)
    h = rms_norm(x, norm_g) * (1.0 + scale[:, None, :]) + shift[:, None, :]
    return h, gate[:, None, :]


def project_inputs(h, w_in, pos):
    b, s, _ = h.shape
    sizes = [WIDTH_A] * 4 + [WIDTH_B] * 4 + [D_MODEL] * 2
    idx = [int(i) for i in np.cumsum(sizes)[:-1]]
    qa, ka, va, za, qb, kb, vb, zb, ga, gb = jnp.split(h @ w_in, idx, axis=-1)
    qa = rope(qa.reshape(b, s, N_HEADS_A, 2, HEAD_DIM_A), pos)
    ka = rope(ka.reshape(b, s, N_HEADS_A, 2, HEAD_DIM_A), pos)
    va = va.reshape(b, s, N_HEADS_A, 2 * HEAD_DIM_A)
    qb = qb.reshape(b, s, N_HEADS_B, HEAD_DIM_B)
    kb = kb.reshape(b, s, N_HEADS_B, HEAD_DIM_B)
    vb = vb.reshape(b, s, N_HEADS_B, HEAD_DIM_B)
    return qa, ka, va, za, qb, kb, vb, zb, ga, gb


def diff_lambda(lq1, lk1, lq2, lk2, lam_init):
    f32 = jnp.float32
    return (jnp.exp(jnp.sum(lq1.astype(f32) * lk1.astype(f32)))
            - jnp.exp(jnp.sum(lq2.astype(f32) * lk2.astype(f32))) + lam_init)


def diff_attention(q, k, v, qpos, kpos, lam):
    s = jnp.einsum('bqhcd,bkhcd->bhcqk', q, k,
                   preferred_element_type=jnp.float32) * (HEAD_DIM_A ** -0.5)
    mask = (kpos // CHUNK)[None, :] <= (qpos // CHUNK)[:, None]
    p = jax.nn.softmax(jnp.where(mask, s, -jnp.inf), axis=-1)
    w = p[:, :, 0] - lam * p[:, :, 1]
    o = jnp.einsum('bhqk,bkhe->bqhe', w, v, preferred_element_type=jnp.float32)
    return o.astype(v.dtype)


def stick_breaking_attention(q, k, v, qpos, kpos):
    z = jnp.einsum('bqhd,bkhd->bhqk', q, k,
                   preferred_element_type=jnp.float32) * (HEAD_DIM_B ** -0.5)
    mask = kpos[None, :] < qpos[:, None]
    log_1m = jnp.where(mask, jax.nn.log_sigmoid(-z), 0.0)
    log_a = jax.nn.log_sigmoid(z) + lax.cumsum(log_1m, axis=3, reverse=True) - log_1m
    a = jnp.where(mask, jnp.exp(log_a), 0.0)
    o = jnp.einsum('bhqk,bkhd->bqhd', a, v, preferred_element_type=jnp.float32)
    return o.astype(v.dtype)


def merge_branches(oa, ob, za, zb, ga, gb, subln_g, lam_init, w_branch_a, w_branch_b, w_out):
    b, s = oa.shape[:2]
    oa = rms_norm(oa, subln_g) * (1.0 - lam_init)
    ya = (oa.reshape(b, s, WIDTH_A) * jax.nn.silu(za)) @ w_branch_a
    yb = (ob.reshape(b, s, WIDTH_B) * jax.nn.silu(zb)) @ w_branch_b
    return (jax.nn.sigmoid(ga) * ya + jax.nn.sigmoid(gb) * yb) @ w_out


def setup_inputs(seed: int = 0) -> dict:
    key = jax.random.key(seed)
    ks = jax.random.split(key, 24)
    f32 = jnp.float32
    nrm = lambda k, shape, s=1.0: (jax.random.normal(k, shape, f32) * s)
    return {
        'x_prompt': nrm(ks[0], (BATCH, SEQ, D_MODEL)),
        'x_sample': nrm(ks[1], (DEC_BATCH, DEC_SEQ, D_MODEL)),
        'c_prompt': nrm(ks[2], (BATCH, D_MODEL)),
        'c_sample': nrm(ks[3], (DEC_BATCH, D_MODEL)),
        'cache_a_k': nrm(ks[4], (DEPTH, DEC_BATCH, PAST_LEN, N_HEADS_A, 2, HEAD_DIM_A)),
        'cache_a_v': nrm(ks[5], (DEPTH, DEC_BATCH, PAST_LEN, N_HEADS_A, 2 * HEAD_DIM_A)),
        'cache_b_k': nrm(ks[6], (DEPTH, DEC_BATCH, PAST_LEN, N_HEADS_B, HEAD_DIM_B)),
        'cache_b_v': nrm(ks[7], (DEPTH, DEC_BATCH, PAST_LEN, N_HEADS_B, HEAD_DIM_B)),
        'norm_g': 1.0 + nrm(ks[8], (DEPTH, D_MODEL), 0.1),
        'w_ada': nrm(ks[9], (DEPTH, D_MODEL, 3 * D_MODEL), 0.5 * D_MODEL ** -0.5),
        'b_ada': nrm(ks[10], (DEPTH, 3 * D_MODEL), 0.02),
        'w_in': nrm(ks[11], (DEPTH, D_MODEL, IN_WIDTH), D_MODEL ** -0.5),
        'lambda_q1': nrm(ks[12], (DEPTH, HEAD_DIM_A), 0.1),
        'lambda_k1': nrm(ks[13], (DEPTH, HEAD_DIM_A), 0.1),
        'lambda_q2': nrm(ks[14], (DEPTH, HEAD_DIM_A), 0.1),
        'lambda_k2': nrm(ks[15], (DEPTH, HEAD_DIM_A), 0.1),
        'subln_g': 1.0 + nrm(ks[16], (DEPTH, 2 * HEAD_DIM_A), 0.1),
        'w_branch_a': nrm(ks[17], (DEPTH, WIDTH_A, D_MODEL), WIDTH_A ** -0.5),
        'w_branch_b': nrm(ks[18], (DEPTH, WIDTH_B, D_MODEL), WIDTH_B ** -0.5),
        'w_out': nrm(ks[19], (DEPTH, D_MODEL, D_MODEL), D_MODEL ** -0.5),
        'final_g': 1.0 + nrm(ks[20], (D_MODEL,), 0.1),
    }


def reference(x_prompt, x_sample, c_prompt, c_sample, cache_a_k, cache_a_v, cache_b_k, cache_b_v,
              norm_g, w_ada, b_ada, w_in, lambda_q1, lambda_k1, lambda_q2, lambda_k2, subln_g,
              w_branch_a, w_branch_b, w_out, final_g):
    bp, sp, _ = x_prompt.shape
    ns = x_sample.shape[1]
    past = cache_a_k.shape[2]
    pos_p = jnp.arange(sp, dtype=jnp.int32)
    pos_s = past + jnp.arange(ns, dtype=jnp.int32)
    kpos_s = jnp.arange(past + ns, dtype=jnp.int32)
    n_blocks = sp // Q_BLOCK
    xp, xs = x_prompt, x_sample
    pak, pav, pbk, pbv = [], [], [], []
    sak, sav, sbk, sbv = [], [], [], []
    for l in range(DEPTH):
        lam_init = 0.8 - 0.6 * math.exp(-0.3 * l)
        lam = diff_lambda(lambda_q1[l], lambda_k1[l], lambda_q2[l], lambda_k2[l], lam_init)

        h, gate = adaln_modulate(xp, c_prompt, norm_g[l], w_ada[l], b_ada[l])
        qa, ka, va, za, qb, kb, vb, zb, ga, gb = project_inputs(h, w_in[l], pos_p)

        def block(i):
            start = i * Q_BLOCK
            qpos = start + jnp.arange(Q_BLOCK, dtype=jnp.int32)
            oa_blk = diff_attention(lax.dynamic_slice_in_dim(qa, start, Q_BLOCK, axis=1),
                                    ka, va, qpos, pos_p, lam)
            ob_blk = stick_breaking_attention(lax.dynamic_slice_in_dim(qb, start, Q_BLOCK, axis=1),
                                              kb, vb, qpos, pos_p)
            return oa_blk, ob_blk

        oa, ob = lax.map(block, jnp.arange(n_blocks, dtype=jnp.int32))
        oa = jnp.moveaxis(oa, 0, 1).reshape(bp, sp, N_HEADS_A, 2 * HEAD_DIM_A)
        ob = jnp.moveaxis(ob, 0, 1).reshape(bp, sp, N_HEADS_B, HEAD_DIM_B)
        xp = xp + gate * merge_branches(oa, ob, za, zb, ga, gb, subln_g[l], lam_init,
                                        w_branch_a[l], w_branch_b[l], w_out[l])
        pak.append(ka)
        pav.append(va)
        pbk.append(kb)
        pbv.append(vb)

        h, gate = adaln_modulate(xs, c_sample, norm_g[l], w_ada[l], b_ada[l])
        qa, ka, va, za, qb, kb, vb, zb, ga, gb = project_inputs(h, w_in[l], pos_s)
        oa = diff_attention(qa, jnp.concatenate([cache_a_k[l], ka], axis=1),
                            jnp.concatenate([cache_a_v[l], va], axis=1), pos_s, kpos_s, lam)
        ob = stick_breaking_attention(qb, jnp.concatenate([cache_b_k[l], kb], axis=1),
                                      jnp.concatenate([cache_b_v[l], vb], axis=1), pos_s, kpos_s)
        xs = xs + gate * merge_branches(oa, ob, za, zb, ga, gb, subln_g[l], lam_init,
                                        w_branch_a[l], w_branch_b[l], w_out[l])
        sak.append(ka)
        sav.append(va)
        sbk.append(kb)
        sbv.append(vb)

    y_prompt = rms_norm(xp, final_g)
    y_sample = rms_norm(xs, final_g)
    return (y_prompt, y_sample,
            jnp.stack(pak), jnp.stack(pav), jnp.stack(pbk), jnp.stack(pbv),
            jnp.stack(sak), jnp.stack(sav), jnp.stack(sbk), jnp.stack(sbv))
```

```python
import functools
import math

import jax
import jax.numpy as jnp
from jax import lax
from jax.experimental import pallas as pl
from jax.experimental.pallas import tpu as pltpu

F32 = jnp.float32
BF16 = jnp.bfloat16

CHUNK = 64
N_HEADS_A = 4
HEAD_DIM_A = 64
WIDTH_A = N_HEADS_A * 2 * HEAD_DIM_A
N_HEADS_B = 8
HEAD_DIM_B = 64
WIDTH_B = N_HEADS_B * HEAD_DIM_B
ROPE_THETA = 10000.0
NORM_EPS = 1e-6

LANES = 128
HALF = 64
VMEM_LIMIT = 56 * 1024 * 1024
NEG = -1e30
SB_DONE = 104.0

PROJ_TOKENS = 512
ATTN_TILE = 256
CACHE_BLOCK = 256

_NT = (((1,), (1,)), ((), ()))


def _params(n_axes):
    return pltpu.CompilerParams(dimension_semantics=("parallel",) * n_axes,
                                vmem_limit_bytes=VMEM_LIMIT)


def _lane_halves(x):
    lane = lax.broadcasted_iota(jnp.int32, x.shape, x.ndim - 1)
    zero = jnp.zeros_like(x)
    return jnp.where(lane < HALF, x, zero), jnp.where(lane >= HALF, x, zero)


def _softplus(z):
    return jnp.maximum(z, 0.0) + jnp.log(1.0 + jnp.exp(-jnp.abs(z)))


def _suffix_matrix(n):
    j = lax.broadcasted_iota(jnp.int32, (n, n), 0)
    s = lax.broadcasted_iota(jnp.int32, (n, n), 1)
    return (j > s).astype(BF16)


def _ada_kernel(c_ref, w_ref, b_ref, o_ref):
    o_ref[...] = jnp.dot(c_ref[...], w_ref[...], preferred_element_type=F32,
                         precision=lax.Precision.HIGHEST) + b_ref[...]


def _ada(c, w, b):
    n, d = c.shape
    width = w.shape[1]
    tn = 512
    return pl.pallas_call(
        _ada_kernel,
        out_shape=jax.ShapeDtypeStruct((n, width), F32),
        grid=(width // tn,),
        in_specs=[pl.BlockSpec((n, d), lambda j: (0, 0)),
                  pl.BlockSpec((d, tn), lambda j: (0, j)),
                  pl.BlockSpec((1, tn), lambda j: (0, j))],
        out_specs=pl.BlockSpec((n, tn), lambda j: (0, j)),
        compiler_params=_params(1),
        name="ada",
    )(c, w, b.reshape(1, width))


def _rope(x, cos, sin_signed):
    lane = lax.broadcasted_iota(jnp.int32, x.shape, 1)
    first_half = (lane % HALF) < (HALF // 2)
    partner = jnp.where(first_half, pltpu.roll(x, LANES - HALF // 2, 1), pltpu.roll(x, HALF // 2, 1))
    return x * cos + partner * sin_signed


def _proj_kernel(x_ref, mod_ref, g_ref, cos_ref, sin_ref, w_ref,
                 ka32, va32, kb32, vb32, qa16, ka16, va16, qb16, kb16, vb16, sza, szb, sga, sgb):
    bb, ts, d = x_ref.shape
    n = bb * ts
    x = x_ref[...]
    y = x * lax.rsqrt(jnp.mean(x * x, axis=-1, keepdims=True) + NORM_EPS) * g_ref[...]
    mod = mod_ref[...]
    h = (y * (1.0 + mod[:, :, d:2 * d]) + mod[:, :, :d]).reshape(n, d).astype(BF16)
    cos = jnp.broadcast_to(cos_ref[...][None], (bb, ts, LANES)).reshape(n, LANES)
    sin = jnp.broadcast_to(sin_ref[...][None], (bb, ts, LANES)).reshape(n, LANES)

    def cols(start, width):
        return jnp.dot(h, w_ref[:, start:start + width], preferred_element_type=F32)

    def put(ref, val):
        ref[...] = val.reshape(ref.shape).astype(ref.dtype)

    def rope_all(val):
        return jnp.concatenate(
            [_rope(val[:, i * LANES:(i + 1) * LANES], cos, sin) for i in range(WIDTH_A // LANES)], axis=1)

    wa, wb = WIDTH_A, WIDTH_B
    put(qa16, rope_all(cols(0, wa)) * (HEAD_DIM_A ** -0.5))
    ka = rope_all(cols(wa, wa))
    put(ka32, ka)
    put(ka16, ka)
    va = cols(2 * wa, wa)
    put(va32, va)
    put(va16, va)
    za = cols(3 * wa, wa)
    put(sza, za * jax.nn.sigmoid(za))
    off = 4 * wa
    put(qb16, cols(off, wb) * (HEAD_DIM_B ** -0.5))
    kb = cols(off + wb, wb)
    put(kb32, kb)
    put(kb16, kb)
    vb = cols(off + 2 * wb, wb)
    put(vb32, vb)
    put(vb16, vb)
    zb = cols(off + 3 * wb, wb)
    put(szb, zb * jax.nn.sigmoid(zb))
    off += 4 * wb
    put(sga, jax.nn.sigmoid(cols(off, d)))
    put(sgb, jax.nn.sigmoid(cols(off + d, d)))


def _proj(x, mod, norm_g, cos, sin, w16, bb, ts):
    b, s, d = x.shape
    grid = (b // bb, s // ts)
    tok = lambda width: pl.BlockSpec((bb, ts, width), lambda i, j: (i, j, 0))
    shape = lambda width, dt: jax.ShapeDtypeStruct((b, s, width), dt)
    widths32 = [WIDTH_A, WIDTH_A, WIDTH_B, WIDTH_B]
    widths16 = [WIDTH_A] * 3 + [WIDTH_B] * 3 + [WIDTH_A, WIDTH_B, d, d]
    return pl.pallas_call(
        _proj_kernel,
        out_shape=[shape(w, F32) for w in widths32] + [shape(w, BF16) for w in widths16],
        grid=grid,
        in_specs=[tok(d),
                  pl.BlockSpec((bb, 1, 3 * d), lambda i, j: (i, 0, 0)),
                  pl.BlockSpec((1, 1, d), lambda i, j: (0, 0, 0)),
                  pl.BlockSpec((ts, LANES), lambda i, j: (j, 0)),
                  pl.BlockSpec((ts, LANES), lambda i, j: (j, 0)),
                  pl.BlockSpec(w16.shape, lambda i, j: (0, 0), pipeline_mode=pl.Buffered(1))],
        out_specs=[tok(w) for w in widths32 + widths16],
        compiler_params=_params(2),
        name="proj",
    )(x, mod, norm_g.reshape(1, 1, d), cos, sin, w16)


def _diff_lambda(lam_ref, lam_init):
    lv = lam_ref[...]
    a = jnp.sum(lv[0:1] * lv[1:2], axis=1, keepdims=True)
    b = jnp.sum(lv[2:3] * lv[3:4], axis=1, keepdims=True)
    return jnp.exp(a) - jnp.exp(b) + lam_init


def _diff_finish(o0, o1, lam, g, lam_init):
    o = o0 - lam * o1
    on = o * lax.rsqrt(jnp.mean(o * o, axis=1, keepdims=True) + NORM_EPS) * g
    return on * (1.0 - lam_init)


def _diff_kernel(lam_ref, q_ref, k_ref, v_ref, g_ref, o_ref, m_ref, l_ref, acc_ref, *, tile, lam_init):
    qi = pl.program_id(2)
    qc = _lane_halves(q_ref[0])
    m_ref[...] = jnp.full(m_ref.shape, NEG, F32)
    l_ref[...] = jnp.zeros(l_ref.shape, F32)
    acc_ref[...] = jnp.zeros(acc_ref.shape, F32)

    def step(kb, mask):
        start = pl.multiple_of(kb * tile, tile)
        k = k_ref[0, pl.ds(start, tile), :]
        v = v_ref[0, pl.ds(start, tile), :]
        for c in range(2):
            s = lax.dot_general(qc[c], k, _NT, preferred_element_type=F32)
            if mask is not None:
                s = jnp.where(mask, s, NEG)
            m_prev = m_ref[c]
            m_next = jnp.maximum(m_prev, jnp.max(s, axis=1, keepdims=True))
            alpha = jnp.exp(m_prev - m_next)
            p = jnp.exp(s - jnp.tile(m_next, (1, tile // LANES)))
            l_ref[c] = alpha * l_ref[c] + jnp.sum(p, axis=1, keepdims=True)
            acc_ref[c] = alpha * acc_ref[c] + jnp.dot(p.astype(BF16), v, preferred_element_type=F32)
            m_ref[c] = m_next

    def full_block(kb, carry):
        step(kb, None)
        return carry

    lax.fori_loop(0, qi, full_block, 0)
    row = lax.broadcasted_iota(jnp.int32, (tile, tile), 0)
    col = lax.broadcasted_iota(jnp.int32, (tile, tile), 1)
    step(qi, (col // CHUNK) <= (row // CHUNK))

    lam = _diff_lambda(lam_ref, lam_init)
    o = _diff_finish(acc_ref[0] / l_ref[0], acc_ref[1] / l_ref[1], lam, g_ref[...], lam_init)
    o_ref[0] = o.astype(o_ref.dtype)


def _diff_attn(lamv, q, k, v, subln_g, lam_init):
    b, s, _ = q.shape
    t = ATTN_TILE
    kv_spec = pl.BlockSpec((1, s, LANES), lambda bi, h, qi: (bi, 0, h))
    q_spec = pl.BlockSpec((1, t, LANES), lambda bi, h, qi: (bi, qi, h))
    return pl.pallas_call(
        functools.partial(_diff_kernel, tile=t, lam_init=lam_init),
        out_shape=jax.ShapeDtypeStruct((b, s, WIDTH_A), BF16),
        grid=(b, N_HEADS_A, s // t),
        in_specs=[pl.BlockSpec(lamv.shape, lambda bi, h, qi: (0, 0)),
                  q_spec, kv_spec, kv_spec,
                  pl.BlockSpec((1, LANES), lambda bi, h, qi: (0, 0))],
        out_specs=q_spec,
        scratch_shapes=[pltpu.VMEM((2, t, LANES), F32)] * 3,
        compiler_params=_params(3),
        name="diff_attn",
    )(lamv, q, k, v, subln_g.reshape(1, LANES))


def _sb_block(z, u, cs, mask):
    sp = _softplus(z)
    if mask is not None:
        sp = jnp.where(mask, sp, 0.0)
    rex = jnp.dot(sp.astype(BF16), u, preferred_element_type=F32)
    a = jnp.exp((z - sp) - rex - cs)
    if mask is not None:
        a = jnp.where(mask, a, 0.0)
    return a, jnp.sum(sp, axis=1, keepdims=True)


def _sb_kernel(q_ref, k_ref, v_ref, u_ref, o_ref, cs_ref, acc_ref, *, tile):
    qi = pl.program_id(2)
    qe = _lane_halves(q_ref[0])
    cs_ref[...] = jnp.zeros(cs_ref.shape, F32)
    acc_ref[...] = jnp.zeros(acc_ref.shape, F32)
    lane = lax.broadcasted_iota(jnp.int32, (tile, LANES), 1)

    def step(kb, mask):
        start = pl.multiple_of(kb * tile, tile)
        k = k_ref[0, pl.ds(start, tile), :]
        v = v_ref[0, pl.ds(start, tile), :]
        pv = []
        for e in range(2):
            z = lax.dot_general(qe[e], k, _NT, preferred_element_type=F32)
            cs = cs_ref[e]
            a, sp_sum = _sb_block(z, u_ref[...], jnp.tile(cs, (1, tile // LANES)), mask)
            pv.append(jnp.dot(a.astype(BF16), v, preferred_element_type=F32))
            cs_ref[e] = cs + sp_sum
        acc_ref[...] += jnp.where(lane < HALF, pv[0], pv[1])

    def least_decay():
        return jnp.min(jnp.minimum(cs_ref[0], cs_ref[1]))

    row = lax.broadcasted_iota(jnp.int32, (tile, tile), 0)
    col = lax.broadcasted_iota(jnp.int32, (tile, tile), 1)
    step(qi, col < row)

    def more(carry):
        kb, decay = carry
        return jnp.logical_and(kb >= 0, decay < SB_DONE)

    def full_block(carry):
        kb, _ = carry
        step(kb, None)
        return kb - 1, least_decay()

    lax.while_loop(more, full_block, (qi - 1, least_decay()))
    o_ref[0] = acc_ref[...].astype(o_ref.dtype)


def _sb_attn(q, k, v):
    b, s, _ = q.shape
    t = ATTN_TILE
    kv_spec = pl.BlockSpec((1, s, LANES), lambda bi, h, qi: (bi, 0, h))
    q_spec = pl.BlockSpec((1, t, LANES), lambda bi, h, qi: (bi, qi, h))
    return pl.pallas_call(
        functools.partial(_sb_kernel, tile=t),
        out_shape=jax.ShapeDtypeStruct((b, s, WIDTH_B), BF16),
        grid=(b, WIDTH_B // LANES, s // t),
        in_specs=[q_spec, kv_spec, kv_spec, pl.BlockSpec((t, t), lambda bi, h, qi: (0, 0))],
        out_specs=q_spec,
        scratch_shapes=[pltpu.VMEM((2, t, LANES), F32), pltpu.VMEM((t, LANES), F32)],
        compiler_params=_params(3),
        name="sb_attn",
    )(q, k, v, _suffix_matrix(t))


def _diff_sample_kernel(lam_ref, q_ref, kn_ref, vn_ref, ck_ref, cv_ref, g_ref, o_ref, *, past, lam_init):
    ns = q_ref.shape[1]
    lam = _diff_lambda(lam_ref, lam_init)
    qpos = past + lax.broadcasted_iota(jnp.int32, (2 * ns, 1), 0) % ns
    cache_ok = (lax.broadcasted_iota(jnp.int32, (2 * ns, past), 1) // CHUNK) <= (qpos // CHUNK)
    new_ok = ((past + lax.broadcasted_iota(jnp.int32, (2 * ns, ns), 1)) // CHUNK) <= (qpos // CHUNK)
    for h in range(N_HEADS_A):
        hs = slice(h * LANES, (h + 1) * LANES)
        qq = jnp.concatenate(_lane_halves(q_ref[0, :, hs]), axis=0)
        kc = ck_ref[0, :, hs].astype(BF16)
        vc = cv_ref[0, :, hs].astype(BF16)
        s_c = jnp.where(cache_ok, lax.dot_general(qq, kc, _NT, preferred_element_type=F32), NEG)
        s_n = jnp.where(new_ok, lax.dot_general(qq, kn_ref[0, :, hs], _NT, preferred_element_type=F32), NEG)
        m = jnp.maximum(jnp.max(s_c, axis=1, keepdims=True), jnp.max(s_n, axis=1, keepdims=True))
        p_c = jnp.exp(s_c - m)
        p_n = jnp.exp(s_n - m)
        l = jnp.sum(p_c, axis=1, keepdims=True) + jnp.sum(p_n, axis=1, keepdims=True)
        acc = (jnp.dot(p_c.astype(BF16), vc, preferred_element_type=F32)
               + jnp.dot(p_n.astype(BF16), vn_ref[0, :, hs], preferred_element_type=F32))
        o = acc / l
        o_ref[0, :, hs] = _diff_finish(o[:ns], o[ns:], lam, g_ref[...], lam_init).astype(o_ref.dtype)


def _diff_sample(lamv, q, kn, vn, ck, cv, subln_g, lam_init):
    b, ns, _ = q.shape
    past = ck.shape[1]
    new = pl.BlockSpec((1, ns, WIDTH_A), lambda i: (i, 0, 0))
    cache = pl.BlockSpec((1, past, WIDTH_A), lambda i: (i, 0, 0))
    return pl.pallas_call(
        functools.partial(_diff_sample_kernel, past=past, lam_init=lam_init),
        out_shape=jax.ShapeDtypeStruct((b, ns, WIDTH_A), BF16),
        grid=(b,),
        in_specs=[pl.BlockSpec(lamv.shape, lambda i: (0, 0)), new, new, new, cache, cache,
                  pl.BlockSpec((1, LANES), lambda i: (0, 0))],
        out_specs=new,
        compiler_params=_params(1),
        name="diff_sample",
    )(lamv, q, kn, vn, ck, cv, subln_g.reshape(1, LANES))


def _sb_sample_kernel(q_ref, kn_ref, vn_ref, ck_ref, cv_ref, un_ref, uc_ref, o_ref, *, past):
    ns = q_ref.shape[1]
    blk = CACHE_BLOCK
    t = lax.broadcasted_iota(jnp.int32, (2 * ns, ns), 0) % ns
    new_ok = lax.broadcasted_iota(jnp.int32, (2 * ns, ns), 1) < t
    lane = lax.broadcasted_iota(jnp.int32, (ns, LANES), 1)
    for hp in range(WIDTH_B // LANES):
        hs = slice(hp * LANES, (hp + 1) * LANES)
        qq = jnp.concatenate(_lane_halves(q_ref[0, :, hs]), axis=0)
        z_n = lax.dot_general(qq, kn_ref[0, :, hs], _NT, preferred_element_type=F32)
        a, cs = _sb_block(z_n, un_ref[...], 0.0, new_ok)
        acc = jnp.dot(a.astype(BF16), vn_ref[0, :, hs], preferred_element_type=F32)
        for kb in reversed(range(past // blk)):
            ks = slice(kb * blk, (kb + 1) * blk)
            z = lax.dot_general(qq, ck_ref[0, ks, hs].astype(BF16), _NT, preferred_element_type=F32)
            a, sp_sum = _sb_block(z, uc_ref[...], cs, None)
            acc += jnp.dot(a.astype(BF16), cv_ref[0, ks, hs].astype(BF16), preferred_element_type=F32)
            cs = cs + sp_sum
        o_ref[0, :, hs] = jnp.where(lane < HALF, acc[:ns], acc[ns:]).astype(o_ref.dtype)


def _sb_sample(q, kn, vn, ck, cv):
    b, ns, _ = q.shape
    past = ck.shape[1]
    new = pl.BlockSpec((1, ns, WIDTH_B), lambda i: (i, 0, 0))
    cache = pl.BlockSpec((1, past, WIDTH_B), lambda i: (i, 0, 0))
    const = lambda n: pl.BlockSpec((n, n), lambda i: (0, 0))
    return pl.pallas_call(
        functools.partial(_sb_sample_kernel, past=past),
        out_shape=jax.ShapeDtypeStruct((b, ns, WIDTH_B), BF16),
        grid=(b,),
        in_specs=[new, new, new, cache, cache, const(ns), const(CACHE_BLOCK)],
        out_specs=new,
        compiler_params=_params(1),
        name="sb_sample",
    )(q, kn, vn, ck, cv, _suffix_matrix(ns), _suffix_matrix(CACHE_BLOCK))


def _merge_kernel(x_ref, mod_ref, oa_ref, ob_ref, sza_ref, szb_ref, sga_ref, sgb_ref,
                  wa_ref, wb_ref, wo_ref, fg_ref, y_ref, *, final_norm):
    bb, ts, d = x_ref.shape
    n = bb * ts
    flat = lambda ref: ref[...].reshape(n, ref.shape[-1])
    ya = jnp.dot(flat(oa_ref) * flat(sza_ref), wa_ref[...], preferred_element_type=F32)
    yb = jnp.dot(flat(ob_ref) * flat(szb_ref), wb_ref[...], preferred_element_type=F32)
    mixed = flat(sga_ref).astype(F32) * ya + flat(sgb_ref).astype(F32) * yb
    out = jnp.dot(mixed.astype(BF16), wo_ref[...], preferred_element_type=F32)
    xn = x_ref[...] + mod_ref[...][:, :, 2 * d:] * out.reshape(bb, ts, d)
    if final_norm:
        xn = xn * lax.rsqrt(jnp.mean(xn * xn, axis=-1, keepdims=True) + NORM_EPS) * fg_ref[...]
    y_ref[...] = xn


def _merge(x, mod, oa, ob, sza, szb, sga, sgb, wa16, wb16, wo16, final_g, bb, ts, final_norm):
    b, s, d = x.shape
    tok = lambda width: pl.BlockSpec((bb, ts, width), lambda i, j: (i, j, 0))
    const = lambda arr: pl.BlockSpec(arr.shape, lambda i, j: (0,) * arr.ndim)
    fg = final_g.reshape(1, 1, d)
    return pl.pallas_call(
        functools.partial(_merge_kernel, final_norm=final_norm),
        out_shape=jax.ShapeDtypeStruct((b, s, d), F32),
        grid=(b // bb, s // ts),
        in_specs=[tok(d), pl.BlockSpec((bb, 1, 3 * d), lambda i, j: (i, 0, 0)),
                  tok(WIDTH_A), tok(WIDTH_B), tok(WIDTH_A), tok(WIDTH_B), tok(d), tok(d),
                  const(wa16), const(wb16), const(wo16), const(fg)],
        out_specs=tok(d),
        compiler_params=_params(2),
        name="merge",
    )(x, mod, oa, ob, sza, szb, sga, sgb, wa16, wb16, wo16, fg)


def _rope_tables(pos):
    half = HEAD_DIM_A // 2
    inv = ROPE_THETA ** (-jnp.arange(half, dtype=F32) / half)
    ang = pos.astype(F32)[:, None] * inv[None, :]
    cos, sin = jnp.cos(ang), jnp.sin(ang)
    reps = LANES // HEAD_DIM_A
    return (jnp.tile(jnp.concatenate([cos, cos], axis=1), (1, reps)),
            jnp.tile(jnp.concatenate([-sin, sin], axis=1), (1, reps)))


def kernel(x_prompt, x_sample, c_prompt, c_sample, cache_a_k, cache_a_v, cache_b_k, cache_b_v, norm_g, w_ada, b_ada, w_in, lambda_q1, lambda_k1, lambda_q2, lambda_k2, subln_g, w_branch_a, w_branch_b, w_out, final_g):
    depth = w_in.shape[0]
    bp, sp, d = x_prompt.shape
    bs, ns, _ = x_sample.shape
    past = cache_a_k.shape[2]
    assert sp % PROJ_TOKENS == 0 and sp % ATTN_TILE == 0 and past % CACHE_BLOCK == 0
    assert ns % 16 == 0 and subln_g.shape[1] == LANES

    cos_p, sin_p = _rope_tables(jnp.arange(sp, dtype=jnp.int32))
    cos_s, sin_s = _rope_tables(past + jnp.arange(ns, dtype=jnp.int32))
    c_all = jnp.concatenate([c_prompt, c_sample], axis=0)

    xp, xs = x_prompt, x_sample
    caches = [[], [], [], [], [], [], [], []]
    for l in range(depth):
        lam_init = 0.8 - 0.6 * math.exp(-0.3 * l)
        final = l == depth - 1
        mod = _ada(c_all, w_ada[l], b_ada[l])
        mod_p = mod[:bp].reshape(bp, 1, 3 * d)
        mod_s = mod[bp:].reshape(bs, 1, 3 * d)
        w16 = w_in[l].astype(BF16)
        wa16, wb16, wo16 = (w[l].astype(BF16) for w in (w_branch_a, w_branch_b, w_out))
        lamv = jnp.stack([lambda_q1[l], lambda_k1[l], lambda_q2[l], lambda_k2[l]])

        (ka32, va32, kb32, vb32, qa, ka, va, qb, kb, vb, sza, szb, sga, sgb) = _proj(
            xp, mod_p, norm_g[l], cos_p, sin_p, w16, 1, PROJ_TOKENS)
        oa = _diff_attn(lamv, qa, ka, va, subln_g[l], lam_init)
        ob = _sb_attn(qb, kb, vb)
        xp = _merge(xp, mod_p, oa, ob, sza, szb, sga, sgb, wa16, wb16, wo16, final_g, 1, PROJ_TOKENS, final)
        for dst, val in zip(caches[:4], (ka32, va32, kb32, vb32)):
            dst.append(val)

        (ka32, va32, kb32, vb32, qa, ka, va, qb, kb, vb, sza, szb, sga, sgb) = _proj(
            xs, mod_s, norm_g[l], cos_s, sin_s, w16, bs, ns)
        oa = _diff_sample(lamv, qa, ka, va, cache_a_k[l].reshape(bs, past, WIDTH_A),
                          cache_a_v[l].reshape(bs, past, WIDTH_A), subln_g[l], lam_init)
        ob = _sb_sample(qb, kb, vb, cache_b_k[l].reshape(bs, past, WIDTH_B),
                        cache_b_v[l].reshape(bs, past, WIDTH_B))
        xs = _merge(xs, mod_s, oa, ob, sza, szb, sga, sgb, wa16, wb16, wo16, final_g, bs, ns, final)
        for dst, val in zip(caches[4:], (ka32, va32, kb32, vb32)):
            dst.append(val)

    def stacked(vals, shape_tail):
        arr = vals[0][None] if depth == 1 else jnp.stack(vals)
        return arr.reshape(arr.shape[:3] + shape_tail)

    tails = [(N_HEADS_A, 2, HEAD_DIM_A), (N_HEADS_A, 2 * HEAD_DIM_A),
             (N_HEADS_B, HEAD_DIM_B), (N_HEADS_B, HEAD_DIM_B)] * 2
    return (xp, xs) + tuple(stacked(v, t) for v, t in zip(caches, tails))
```

```python
import functools
import math

import jax
import jax.numpy as jnp
from jax import lax
from jax.experimental import pallas as pl
from jax.experimental.pallas import tpu as pltpu

F32 = jnp.float32
BF16 = jnp.bfloat16

CHUNK = 64
N_HEADS_A = 4
HEAD_DIM_A = 64
WIDTH_A = N_HEADS_A * 2 * HEAD_DIM_A
N_HEADS_B = 8
HEAD_DIM_B = 64
WIDTH_B = N_HEADS_B * HEAD_DIM_B
ROPE_THETA = 10000.0
NORM_EPS = 1e-6

LANES = 128
HALF = 64
VMEM_LIMIT = 56 * 1024 * 1024
NEG = -1e30
SB_DONE = 150.0

PROJ_TOKENS = 512
DIFF_TILE = 512
SB_TILE = 256
LOG2E = math.log2(math.e)
CACHE_BLOCK = 256

_NT = (((1,), (1,)), ((), ()))


def _params(n_axes):
    return pltpu.CompilerParams(dimension_semantics=("parallel",) * n_axes,
                                vmem_limit_bytes=VMEM_LIMIT)


def _lane_halves(x):
    lane = lax.broadcasted_iota(jnp.int32, x.shape, x.ndim - 1)
    zero = jnp.zeros_like(x)
    return jnp.where(lane < HALF, x, zero), jnp.where(lane >= HALF, x, zero)


def _softplus2(z):
    return jnp.maximum(z, 0.0) + jnp.log2(1.0 + jnp.exp2(jnp.minimum(z, -z)))


def _suffix_matrix(n):
    j = lax.broadcasted_iota(jnp.int32, (n, n), 0)
    s = lax.broadcasted_iota(jnp.int32, (n, n), 1)
    return (j > s).astype(BF16)


def _ada_kernel(c_ref, w_ref, b_ref, o_ref):
    o_ref[...] = jnp.dot(c_ref[...], w_ref[...], preferred_element_type=F32,
                         precision=lax.Precision.HIGHEST) + b_ref[...]


def _ada(c, w, b):
    n, d = c.shape
    width = w.shape[1]
    tn = 512
    return pl.pallas_call(
        _ada_kernel,
        out_shape=jax.ShapeDtypeStruct((n, width), F32),
        grid=(width // tn,),
        in_specs=[pl.BlockSpec((n, d), lambda j: (0, 0)),
                  pl.BlockSpec((d, tn), lambda j: (0, j)),
                  pl.BlockSpec((1, tn), lambda j: (0, j))],
        out_specs=pl.BlockSpec((n, tn), lambda j: (0, j)),
        compiler_params=_params(1),
        name="ada",
    )(c, w, b.reshape(1, width))


def _rope(x, cos, sin_signed):
    lane = lax.broadcasted_iota(jnp.int32, x.shape, 1)
    first_half = (lane % HALF) < (HALF // 2)
    partner = jnp.where(first_half, pltpu.roll(x, LANES - HALF // 2, 1), pltpu.roll(x, HALF // 2, 1))
    return x * cos + partner * sin_signed


def _proj_kernel(x_ref, mod_ref, g_ref, cos_ref, sin_ref, w_ref,
                 ka32, va32, kb32, vb32, qa16, ka16, va16, qb16, kb16, vb16, sza, szb, sga, sgb):
    bb, ts, d = x_ref.shape
    n = bb * ts
    x = x_ref[...]
    y = x * lax.rsqrt(jnp.mean(x * x, axis=-1, keepdims=True) + NORM_EPS) * g_ref[...]
    mod = mod_ref[...]
    h = (y * (1.0 + mod[:, :, d:2 * d]) + mod[:, :, :d]).reshape(n, d).astype(BF16)
    cos = jnp.broadcast_to(cos_ref[...][None], (bb, ts, LANES)).reshape(n, LANES)
    sin = jnp.broadcast_to(sin_ref[...][None], (bb, ts, LANES)).reshape(n, LANES)

    def cols(start, width):
        return jnp.dot(h, w_ref[:, start:start + width], preferred_element_type=F32)

    def put(ref, val):
        ref[...] = val.reshape(ref.shape).astype(ref.dtype)

    def rope_all(val):
        return jnp.concatenate(
            [_rope(val[:, i * LANES:(i + 1) * LANES], cos, sin) for i in range(WIDTH_A // LANES)], axis=1)

    wa, wb = WIDTH_A, WIDTH_B
    put(qa16, rope_all(cols(0, wa)) * (HEAD_DIM_A ** -0.5 * LOG2E))
    ka = rope_all(cols(wa, wa))
    put(ka32, ka)
    put(ka16, ka)
    va = cols(2 * wa, wa)
    put(va32, va)
    put(va16, va)
    za = cols(3 * wa, wa)
    put(sza, za * jax.nn.sigmoid(za))
    off = 4 * wa
    put(qb16, cols(off, wb) * (HEAD_DIM_B ** -0.5 * LOG2E))
    kb = cols(off + wb, wb)
    put(kb32, kb)
    put(kb16, kb)
    vb = cols(off + 2 * wb, wb)
    put(vb32, vb)
    put(vb16, vb)
    zb = cols(off + 3 * wb, wb)
    put(szb, zb * jax.nn.sigmoid(zb))
    off += 4 * wb
    put(sga, jax.nn.sigmoid(cols(off, d)))
    put(sgb, jax.nn.sigmoid(cols(off + d, d)))


def _proj(x, mod, norm_g, cos, sin, w16, bb, ts):
    b, s, d = x.shape
    grid = (b // bb, s // ts)
    tok = lambda width: pl.BlockSpec((bb, ts, width), lambda i, j: (i, j, 0))
    shape = lambda width, dt: jax.ShapeDtypeStruct((b, s, width), dt)
    widths32 = [WIDTH_A, WIDTH_A, WIDTH_B, WIDTH_B]
    widths16 = [WIDTH_A] * 3 + [WIDTH_B] * 3 + [WIDTH_A, WIDTH_B, d, d]
    return pl.pallas_call(
        _proj_kernel,
        out_shape=[shape(w, F32) for w in widths32] + [shape(w, BF16) for w in widths16],
        grid=grid,
        in_specs=[tok(d),
                  pl.BlockSpec((bb, 1, 3 * d), lambda i, j: (i, 0, 0)),
                  pl.BlockSpec((1, 1, d), lambda i, j: (0, 0, 0)),
                  pl.BlockSpec((ts, LANES), lambda i, j: (j, 0)),
                  pl.BlockSpec((ts, LANES), lambda i, j: (j, 0)),
                  pl.BlockSpec(w16.shape, lambda i, j: (0, 0), pipeline_mode=pl.Buffered(1))],
        out_specs=[tok(w) for w in widths32 + widths16],
        compiler_params=_params(2),
        name="proj",
    )(x, mod, norm_g.reshape(1, 1, d), cos, sin, w16)


def _diff_lambda(lam_ref, lam_init):
    lv = lam_ref[...]
    a = jnp.sum(lv[0:1] * lv[1:2], axis=1, keepdims=True)
    b = jnp.sum(lv[2:3] * lv[3:4], axis=1, keepdims=True)
    return jnp.exp(a) - jnp.exp(b) + lam_init


def _diff_finish(o0, o1, lam, g, lam_init):
    o = o0 - lam * o1
    on = o * lax.rsqrt(jnp.mean(o * o, axis=1, keepdims=True) + NORM_EPS) * g
    return on * (1.0 - lam_init)


def _diff_kernel(lam_ref, q_ref, k_ref, v_ref, g_ref, o_ref, sa_ref, sb_ref, m_ref, l_ref, acc_ref,
                 *, tile, lam_init):
    qi = pl.program_id(2)
    qc = _lane_halves(q_ref[0])
    m_ref[...] = jnp.full(m_ref.shape, NEG, F32)
    l_ref[...] = jnp.zeros(l_ref.shape, F32)
    acc_ref[...] = jnp.zeros(acc_ref.shape, F32)

    def produce(kb, s_ref):
        k = k_ref[0, pl.ds(pl.multiple_of(kb * tile, tile), tile), :]
        for c in range(2):
            s_ref[c] = lax.dot_general(qc[c], k, _NT, preferred_element_type=F32)

    def consume(kb, s_ref, mask):
        v = v_ref[0, pl.ds(pl.multiple_of(kb * tile, tile), tile), :]
        for c in range(2):
            s = s_ref[c]
            if mask is not None:
                s = jnp.where(mask, s, NEG)
            m_prev = m_ref[c]
            m_next = jnp.maximum(m_prev, jnp.max(s, axis=1, keepdims=True))
            alpha = jnp.exp2(m_prev - m_next)
            p = jnp.exp2(s - jnp.tile(m_next, (1, tile // LANES)))
            l_ref[c] = alpha * l_ref[c] + jnp.sum(p, axis=1, keepdims=True)
            acc_ref[c] = alpha * acc_ref[c] + jnp.dot(p.astype(BF16), v, preferred_element_type=F32)
            m_ref[c] = m_next

    def two_blocks(i, carry):
        kb = 2 * i
        produce(kb + 1, sb_ref)
        consume(kb, sa_ref, None)
        produce(kb + 2, sa_ref)
        consume(kb + 1, sb_ref, None)
        return carry

    produce(0, sa_ref)
    lax.fori_loop(0, qi // 2, two_blocks, 0)
    row = lax.broadcasted_iota(jnp.int32, (tile, tile), 0)
    col = lax.broadcasted_iota(jnp.int32, (tile, tile), 1)
    diagonal = (col // CHUNK) <= (row // CHUNK)

    @pl.when(qi % 2 == 1)
    def _():
        produce(qi, sb_ref)
        consume(qi - 1, sa_ref, None)
        consume(qi, sb_ref, diagonal)

    @pl.when(qi % 2 == 0)
    def _():
        consume(qi, sa_ref, diagonal)

    lam = _diff_lambda(lam_ref, lam_init)
    o = _diff_finish(acc_ref[0] / l_ref[0], acc_ref[1] / l_ref[1], lam, g_ref[...], lam_init)
    o_ref[0] = o.astype(o_ref.dtype)


def _diff_attn(lamv, q, k, v, subln_g, lam_init):
    b, s, _ = q.shape
    t = DIFF_TILE
    kv_spec = pl.BlockSpec((1, s, LANES), lambda bi, h, qi: (bi, 0, h))
    q_spec = pl.BlockSpec((1, t, LANES), lambda bi, h, qi: (bi, qi, h))
    return pl.pallas_call(
        functools.partial(_diff_kernel, tile=t, lam_init=lam_init),
        out_shape=jax.ShapeDtypeStruct((b, s, WIDTH_A), BF16),
        grid=(b, N_HEADS_A, s // t),
        in_specs=[pl.BlockSpec(lamv.shape, lambda bi, h, qi: (0, 0)),
                  q_spec, kv_spec, kv_spec,
                  pl.BlockSpec((1, LANES), lambda bi, h, qi: (0, 0))],
        out_specs=q_spec,
        scratch_shapes=[pltpu.VMEM((2, t, t), F32)] * 2 + [pltpu.VMEM((2, t, LANES), F32)] * 3,
        compiler_params=_params(3),
        name="diff_attn",
    )(lamv, q, k, v, subln_g.reshape(1, LANES))


def _sb_block(z, u, cs, mask):
    sp = _softplus2(z)
    if mask is not None:
        sp = jnp.where(mask, sp, 0.0)
    rex = jnp.dot(sp.astype(BF16), u, preferred_element_type=F32)
    a = jnp.exp2((z - sp) - rex - cs)
    if mask is not None:
        a = jnp.where(mask, a, 0.0)
    return a, jnp.sum(sp, axis=1, keepdims=True)


def _sb_kernel(q_ref, k_ref, v_ref, u_ref, o_ref, cs_ref, acc_ref, *, tile):
    qi = pl.program_id(2)
    qe = _lane_halves(q_ref[0])
    lane = lax.broadcasted_iota(jnp.int32, (tile, LANES), 1)

    def load(kb):
        start = pl.multiple_of(kb * tile, tile)
        return k_ref[0, pl.ds(start, tile), :], v_ref[0, pl.ds(start, tile), :]

    def block(e, k, v, cs, mask):
        z = lax.dot_general(qe[e], k, _NT, preferred_element_type=F32)
        a, sp_sum = _sb_block(z, u_ref[...], jnp.tile(cs, (1, tile // LANES)), mask)
        return jnp.dot(a.astype(BF16), v, preferred_element_type=F32), sp_sum

    row = lax.broadcasted_iota(jnp.int32, (tile, tile), 0)
    col = lax.broadcasted_iota(jnp.int32, (tile, tile), 1)
    k_d, v_d = load(qi)
    k_p, v_p = load(jnp.maximum(qi - 1, 0))
    has_prev = qi > 0
    v_p = jnp.where(has_prev, v_p, jnp.zeros_like(v_p))
    pv = []
    for e in range(2):
        pv_d, sum_d = block(e, k_d, v_d, jnp.zeros((tile, LANES), F32), col < row)
        sum_d = jnp.broadcast_to(sum_d, (tile, LANES))
        pv_p, sum_p = block(e, k_p, v_p, sum_d, None)
        pv.append(pv_d + pv_p)
        cs_ref[e] = sum_d + jnp.where(has_prev, sum_p, 0.0)
    acc_ref[...] = jnp.where(lane < HALF, pv[0], pv[1])

    def least_decay():
        return jnp.min(jnp.minimum(cs_ref[0], cs_ref[1]))

    def more(carry):
        kb, decay = carry
        return jnp.logical_and(kb >= 0, decay < SB_DONE)

    def full_block(carry):
        kb, _ = carry
        k, v = load(kb)
        pv = []
        for e in range(2):
            cs = cs_ref[e]
            pv_e, sp_sum = block(e, k, v, cs, None)
            pv.append(pv_e)
            cs_ref[e] = cs + sp_sum
        acc_ref[...] += jnp.where(lane < HALF, pv[0], pv[1])
        return kb - 1, least_decay()

    lax.while_loop(more, full_block, (qi - 2, least_decay()))
    o_ref[0] = acc_ref[...].astype(o_ref.dtype)


def _sb_attn(q, k, v):
    b, s, _ = q.shape
    t = SB_TILE
    kv_spec = pl.BlockSpec((1, s, LANES), lambda bi, h, qi: (bi, 0, h))
    q_spec = pl.BlockSpec((1, t, LANES), lambda bi, h, qi: (bi, qi, h))
    return pl.pallas_call(
        functools.partial(_sb_kernel, tile=t),
        out_shape=jax.ShapeDtypeStruct((b, s, WIDTH_B), BF16),
        grid=(b, WIDTH_B // LANES, s // t),
        in_specs=[q_spec, kv_spec, kv_spec, pl.BlockSpec((t, t), lambda bi, h, qi: (0, 0))],
        out_specs=q_spec,
        scratch_shapes=[pltpu.VMEM((2, t, LANES), F32), pltpu.VMEM((t, LANES), F32)],
        compiler_params=_params(3),
        name="sb_attn",
    )(q, k, v, _suffix_matrix(t))


def _diff_sample_kernel(lam_ref, q_ref, kn_ref, vn_ref, ck_ref, cv_ref, g_ref, o_ref, *, past, lam_init):
    ns = q_ref.shape[1]
    lam = _diff_lambda(lam_ref, lam_init)
    qpos = past + lax.broadcasted_iota(jnp.int32, (2 * ns, 1), 0) % ns
    cache_ok = (lax.broadcasted_iota(jnp.int32, (2 * ns, past), 1) // CHUNK) <= (qpos // CHUNK)
    new_ok = ((past + lax.broadcasted_iota(jnp.int32, (2 * ns, ns), 1)) // CHUNK) <= (qpos // CHUNK)
    for h in range(N_HEADS_A):
        hs = slice(h * LANES, (h + 1) * LANES)
        qq = jnp.concatenate(_lane_halves(q_ref[0, :, hs]), axis=0)
        kc = ck_ref[0, :, hs].astype(BF16)
        vc = cv_ref[0, :, hs].astype(BF16)
        s_c = jnp.where(cache_ok, lax.dot_general(qq, kc, _NT, preferred_element_type=F32), NEG)
        s_n = jnp.where(new_ok, lax.dot_general(qq, kn_ref[0, :, hs], _NT, preferred_element_type=F32), NEG)
        m = jnp.maximum(jnp.max(s_c, axis=1, keepdims=True), jnp.max(s_n, axis=1, keepdims=True))
        p_c = jnp.exp2(s_c - m)
        p_n = jnp.exp2(s_n - m)
        l = jnp.sum(p_c, axis=1, keepdims=True) + jnp.sum(p_n, axis=1, keepdims=True)
        acc = (jnp.dot(p_c.astype(BF16), vc, preferred_element_type=F32)
               + jnp.dot(p_n.astype(BF16), vn_ref[0, :, hs], preferred_element_type=F32))
        o = acc / l
        o_ref[0, :, hs] = _diff_finish(o[:ns], o[ns:], lam, g_ref[...], lam_init).astype(o_ref.dtype)


def _diff_sample(lamv, q, kn, vn, ck, cv, subln_g, lam_init):
    b, ns, _ = q.shape
    past = ck.shape[1]
    new = pl.BlockSpec((1, ns, WIDTH_A), lambda i: (i, 0, 0))
    cache = pl.BlockSpec((1, past, WIDTH_A), lambda i: (i, 0, 0))
    return pl.pallas_call(
        functools.partial(_diff_sample_kernel, past=past, lam_init=lam_init),
        out_shape=jax.ShapeDtypeStruct((b, ns, WIDTH_A), BF16),
        grid=(b,),
        in_specs=[pl.BlockSpec(lamv.shape, lambda i: (0, 0)), new, new, new, cache, cache,
                  pl.BlockSpec((1, LANES), lambda i: (0, 0))],
        out_specs=new,
        compiler_params=_params(1),
        name="diff_sample",
    )(lamv, q, kn, vn, ck, cv, subln_g.reshape(1, LANES))


def _sb_sample_kernel(q_ref, kn_ref, vn_ref, ck_ref, cv_ref, un_ref, uc_ref, o_ref, *, past):
    ns = q_ref.shape[1]
    blk = CACHE_BLOCK
    t = lax.broadcasted_iota(jnp.int32, (2 * ns, ns), 0) % ns
    new_ok = lax.broadcasted_iota(jnp.int32, (2 * ns, ns), 1) < t
    lane = lax.broadcasted_iota(jnp.int32, (ns, LANES), 1)
    for hp in range(WIDTH_B // LANES):
        hs = slice(hp * LANES, (hp + 1) * LANES)
        qq = jnp.concatenate(_lane_halves(q_ref[0, :, hs]), axis=0)
        z_n = lax.dot_general(qq, kn_ref[0, :, hs], _NT, preferred_element_type=F32)
        a, cs = _sb_block(z_n, un_ref[...], 0.0, new_ok)
        acc = jnp.dot(a.astype(BF16), vn_ref[0, :, hs], preferred_element_type=F32)
        for kb in reversed(range(past // blk)):
            ks = slice(kb * blk, (kb + 1) * blk)
            z = lax.dot_general(qq, ck_ref[0, ks, hs].astype(BF16), _NT, preferred_element_type=F32)
            a, sp_sum = _sb_block(z, uc_ref[...], cs, None)
            acc += jnp.dot(a.astype(BF16), cv_ref[0, ks, hs].astype(BF16), preferred_element_type=F32)
            cs = cs + sp_sum
        o_ref[0, :, hs] = jnp.where(lane < HALF, acc[:ns], acc[ns:]).astype(o_ref.dtype)


def _sb_sample(q, kn, vn, ck, cv):
    b, ns, _ = q.shape
    past = ck.shape[1]
    new = pl.BlockSpec((1, ns, WIDTH_B), lambda i: (i, 0, 0))
    cache = pl.BlockSpec((1, past, WIDTH_B), lambda i: (i, 0, 0))
    const = lambda n: pl.BlockSpec((n, n), lambda i: (0, 0))
    return pl.pallas_call(
        functools.partial(_sb_sample_kernel, past=past),
        out_shape=jax.ShapeDtypeStruct((b, ns, WIDTH_B), BF16),
        grid=(b,),
        in_specs=[new, new, new, cache, cache, const(ns), const(CACHE_BLOCK)],
        out_specs=new,
        compiler_params=_params(1),
        name="sb_sample",
    )(q, kn, vn, ck, cv, _suffix_matrix(ns), _suffix_matrix(CACHE_BLOCK))


def _merge_kernel(x_ref, mod_ref, oa_ref, ob_ref, sza_ref, szb_ref, sga_ref, sgb_ref,
                  wa_ref, wb_ref, wo_ref, fg_ref, y_ref, *, final_norm):
    bb, ts, d = x_ref.shape
    n = bb * ts
    flat = lambda ref: ref[...].reshape(n, ref.shape[-1])
    ya = jnp.dot(flat(oa_ref) * flat(sza_ref), wa_ref[...], preferred_element_type=F32)
    yb = jnp.dot(flat(ob_ref) * flat(szb_ref), wb_ref[...], preferred_element_type=F32)
    mixed = flat(sga_ref).astype(F32) * ya + flat(sgb_ref).astype(F32) * yb
    out = jnp.dot(mixed.astype(BF16), wo_ref[...], preferred_element_type=F32)
    xn = x_ref[...] + mod_ref[...][:, :, 2 * d:] * out.reshape(bb, ts, d)
    if final_norm:
        xn = xn * lax.rsqrt(jnp.mean(xn * xn, axis=-1, keepdims=True) + NORM_EPS) * fg_ref[...]
    y_ref[...] = xn


def _merge(x, mod, oa, ob, sza, szb, sga, sgb, wa16, wb16, wo16, final_g, bb, ts, final_norm):
    b, s, d = x.shape
    tok = lambda width: pl.BlockSpec((bb, ts, width), lambda i, j: (i, j, 0))
    const = lambda arr: pl.BlockSpec(arr.shape, lambda i, j: (0,) * arr.ndim)
    fg = final_g.reshape(1, 1, d)
    return pl.pallas_call(
        functools.partial(_merge_kernel, final_norm=final_norm),
        out_shape=jax.ShapeDtypeStruct((b, s, d), F32),
        grid=(b // bb, s // ts),
        in_specs=[tok(d), pl.BlockSpec((bb, 1, 3 * d), lambda i, j: (i, 0, 0)),
                  tok(WIDTH_A), tok(WIDTH_B), tok(WIDTH_A), tok(WIDTH_B), tok(d), tok(d),
                  const(wa16), const(wb16), const(wo16), const(fg)],
        out_specs=tok(d),
        compiler_params=_params(2),
        name="merge",
    )(x, mod, oa, ob, sza, szb, sga, sgb, wa16, wb16, wo16, fg)


def _rope_tables(pos):
    half = HEAD_DIM_A // 2
    inv = ROPE_THETA ** (-jnp.arange(half, dtype=F32) / half)
    ang = pos.astype(F32)[:, None] * inv[None, :]
    cos, sin = jnp.cos(ang), jnp.sin(ang)
    reps = LANES // HEAD_DIM_A
    return (jnp.tile(jnp.concatenate([cos, cos], axis=1), (1, reps)),
            jnp.tile(jnp.concatenate([-sin, sin], axis=1), (1, reps)))


def kernel(x_prompt, x_sample, c_prompt, c_sample, cache_a_k, cache_a_v, cache_b_k, cache_b_v, norm_g, w_ada, b_ada, w_in, lambda_q1, lambda_k1, lambda_q2, lambda_k2, subln_g, w_branch_a, w_branch_b, w_out, final_g):
    depth = w_in.shape[0]
    bp, sp, d = x_prompt.shape
    bs, ns, _ = x_sample.shape
    past = cache_a_k.shape[2]
    assert sp % PROJ_TOKENS == 0 and sp % DIFF_TILE == 0 and sp % SB_TILE == 0 and past % CACHE_BLOCK == 0
    assert ns % 16 == 0 and subln_g.shape[1] == LANES

    cos_p, sin_p = _rope_tables(jnp.arange(sp, dtype=jnp.int32))
    cos_s, sin_s = _rope_tables(past + jnp.arange(ns, dtype=jnp.int32))
    c_all = jnp.concatenate([c_prompt, c_sample], axis=0)

    xp, xs = x_prompt, x_sample
    caches = [[], [], [], [], [], [], [], []]
    for l in range(depth):
        lam_init = 0.8 - 0.6 * math.exp(-0.3 * l)
        final = l == depth - 1
        mod = _ada(c_all, w_ada[l], b_ada[l])
        mod_p = mod[:bp].reshape(bp, 1, 3 * d)
        mod_s = mod[bp:].reshape(bs, 1, 3 * d)
        w16 = w_in[l].astype(BF16)
        wa16, wb16, wo16 = (w[l].astype(BF16) for w in (w_branch_a, w_branch_b, w_out))
        lamv = jnp.stack([lambda_q1[l], lambda_k1[l], lambda_q2[l], lambda_k2[l]])

        (ka32, va32, kb32, vb32, qa, ka, va, qb, kb, vb, sza, szb, sga, sgb) = _proj(
            xp, mod_p, norm_g[l], cos_p, sin_p, w16, 1, PROJ_TOKENS)
        oa = _diff_attn(lamv, qa, ka, va, subln_g[l], lam_init)
        ob = _sb_attn(qb, kb, vb)
        xp = _merge(xp, mod_p, oa, ob, sza, szb, sga, sgb, wa16, wb16, wo16, final_g, 1, PROJ_TOKENS, final)
        for dst, val in zip(caches[:4], (ka32, va32, kb32, vb32)):
            dst.append(val)

        (ka32, va32, kb32, vb32, qa, ka, va, qb, kb, vb, sza, szb, sga, sgb) = _proj(
            xs, mod_s, norm_g[l], cos_s, sin_s, w16, bs, ns)
        oa = _diff_sample(lamv, qa, ka, va, cache_a_k[l].reshape(bs, past, WIDTH_A),
                          cache_a_v[l].reshape(bs, past, WIDTH_A), subln_g[l], lam_init)
        ob = _sb_sample(qb, kb, vb, cache_b_k[l].reshape(bs, past, WIDTH_B),
                        cache_b_v[l].reshape(bs, past, WIDTH_B))
        xs = _merge(xs, mod_s, oa, ob, sza, szb, sga, sgb, wa16, wb16, wo16, final_g, bs, ns, final)
        for dst, val in zip(caches[4:], (ka32, va32, kb32, vb32)):
            dst.append(val)

    def stacked(vals, shape_tail):
        arr = vals[0][None] if depth == 1 else jnp.stack(vals)
        return arr.reshape(arr.shape[:3] + shape_tail)

    tails = [(N_HEADS_A, 2, HEAD_DIM_A), (N_HEADS_A, 2 * HEAD_DIM_A),
             (N_HEADS_B, HEAD_DIM_B), (N_HEADS_B, HEAD_DIM_B)] * 2
    return (xp, xs) + tuple(stacked(v, t) for v, t in zip(caches, tails))
```

```python
import functools
import math

import jax
import jax.numpy as jnp
from jax import lax
from jax.experimental import pallas as pl
from jax.experimental.pallas import tpu as pltpu

F32 = jnp.float32
BF16 = jnp.bfloat16

CHUNK = 64
N_HEADS_A = 4
HEAD_DIM_A = 64
WIDTH_A = N_HEADS_A * 2 * HEAD_DIM_A
N_HEADS_B = 8
HEAD_DIM_B = 64
WIDTH_B = N_HEADS_B * HEAD_DIM_B
ROPE_THETA = 10000.0
NORM_EPS = 1e-6

LANES = 128
HALF = 64
VMEM_LIMIT = 56 * 1024 * 1024
NEG = -1e30
SB_DONE = 150.0

PROJ_TOKENS = 512
DIFF_TILE = 512
SB_TILE = 256
LOG2E = math.log2(math.e)
CACHE_BLOCK = 256

_NT = (((1,), (1,)), ((), ()))


def _params(n_axes):
    return pltpu.CompilerParams(dimension_semantics=("parallel",) * n_axes,
                                vmem_limit_bytes=VMEM_LIMIT)


def _lane_halves(x):
    lane = lax.broadcasted_iota(jnp.int32, x.shape, x.ndim - 1)
    zero = jnp.zeros_like(x)
    return jnp.where(lane < HALF, x, zero), jnp.where(lane >= HALF, x, zero)


def _softplus2(z):
    return jnp.maximum(z, 0.0) + jnp.log2(1.0 + jnp.exp2(jnp.minimum(z, -z)))


def _suffix_matrix(n):
    j = lax.broadcasted_iota(jnp.int32, (n, n), 0)
    s = lax.broadcasted_iota(jnp.int32, (n, n), 1)
    return (j > s).astype(BF16)


def _ada_kernel(c_ref, w_ref, b_ref, o_ref):
    o_ref[...] = jnp.dot(c_ref[...], w_ref[...], preferred_element_type=F32,
                         precision=lax.Precision.HIGHEST) + b_ref[...]


def _ada(c, w, b):
    n, d = c.shape
    width = w.shape[1]
    tn = 512
    return pl.pallas_call(
        _ada_kernel,
        out_shape=jax.ShapeDtypeStruct((n, width), F32),
        grid=(width // tn,),
        in_specs=[pl.BlockSpec((n, d), lambda j: (0, 0)),
                  pl.BlockSpec((d, tn), lambda j: (0, j)),
                  pl.BlockSpec((1, tn), lambda j: (0, j))],
        out_specs=pl.BlockSpec((n, tn), lambda j: (0, j)),
        compiler_params=_params(1),
        name="ada",
    )(c, w, b.reshape(1, width))


def _rope(x, cos, sin_signed):
    lane = lax.broadcasted_iota(jnp.int32, x.shape, 1)
    first_half = (lane % HALF) < (HALF // 2)
    partner = jnp.where(first_half, pltpu.roll(x, LANES - HALF // 2, 1), pltpu.roll(x, HALF // 2, 1))
    return x * cos + partner * sin_signed


def _proj_kernel(x_ref, mod_ref, g_ref, cos_ref, sin_ref, w_ref,
                 ka32, va32, kb32, vb32, qa16, ka16, va16, qb16, kb16, vb16, sza, szb, sga, sgb):
    bb, ts, d = x_ref.shape
    n = bb * ts
    x = x_ref[...]
    y = x * lax.rsqrt(jnp.mean(x * x, axis=-1, keepdims=True) + NORM_EPS) * g_ref[...]
    mod = mod_ref[...]
    h = (y * (1.0 + mod[:, :, d:2 * d]) + mod[:, :, :d]).reshape(n, d).astype(BF16)
    cos = jnp.broadcast_to(cos_ref[...][None], (bb, ts, LANES)).reshape(n, LANES)
    sin = jnp.broadcast_to(sin_ref[...][None], (bb, ts, LANES)).reshape(n, LANES)

    def cols(start, width):
        return jnp.dot(h, w_ref[:, start:start + width], preferred_element_type=F32)

    def put(ref, val):
        if len(ref.shape) == 3:
            ref[...] = val.reshape(ref.shape).astype(ref.dtype)
            return
        src = val.T if len(ref.shape) == 5 else val
        for i in range(ref.shape[1]):
            if len(ref.shape) == 5:
                ref[0, i, 0] = src[i * LANES:(i + 1) * LANES].astype(ref.dtype)
            else:
                ref[0, i] = src[:, i * LANES:(i + 1) * LANES].astype(ref.dtype)

    def rope_all(val):
        return jnp.concatenate(
            [_rope(val[:, i * LANES:(i + 1) * LANES], cos, sin) for i in range(WIDTH_A // LANES)], axis=1)

    wa, wb = WIDTH_A, WIDTH_B
    put(qa16, rope_all(cols(0, wa)) * (HEAD_DIM_A ** -0.5 * LOG2E))
    ka = rope_all(cols(wa, wa))
    put(ka32, ka)
    put(ka16, ka)
    va = cols(2 * wa, wa)
    put(va32, va)
    put(va16, va)
    za = cols(3 * wa, wa)
    put(sza, za * jax.nn.sigmoid(za))
    off = 4 * wa
    put(qb16, cols(off, wb) * (HEAD_DIM_B ** -0.5 * LOG2E))
    kb = cols(off + wb, wb)
    put(kb32, kb)
    put(kb16, kb)
    vb = cols(off + 2 * wb, wb)
    put(vb32, vb)
    put(vb16, vb)
    zb = cols(off + 3 * wb, wb)
    put(szb, zb * jax.nn.sigmoid(zb))
    off += 4 * wb
    put(sga, jax.nn.sigmoid(cols(off, d)))
    put(sgb, jax.nn.sigmoid(cols(off + d, d)))


def _proj(x, mod, norm_g, cos, sin, w16, bb, ts, head_major):
    b, s, d = x.shape
    grid = (b // bb, s // ts)
    tok = lambda width: pl.BlockSpec((bb, ts, width), lambda i, j: (i, j, 0))
    shape = lambda width, dt: jax.ShapeDtypeStruct((b, s, width), dt)
    widths32 = [WIDTH_A, WIDTH_A, WIDTH_B, WIDTH_B]
    widths16 = [WIDTH_A] * 3 + [WIDTH_B] * 3 + [WIDTH_A, WIDTH_B, d, d]
    out_shape = [shape(w, F32) for w in widths32] + [shape(w, BF16) for w in widths16]
    out_specs = [tok(w) for w in widths32 + widths16]
    if head_major:
        assert bb == 1
        for o in range(4, 10):
            groups = widths16[o - 4] // LANES
            out_shape[o] = jax.ShapeDtypeStruct((b, groups, s, LANES), BF16)
            out_specs[o] = pl.BlockSpec((1, groups, ts, LANES), lambda i, j: (i, 0, j, 0))
        out_shape[6] = jax.ShapeDtypeStruct((b, N_HEADS_A, s // ts, LANES, ts), BF16)
        out_specs[6] = pl.BlockSpec((1, N_HEADS_A, 1, LANES, ts), lambda i, j: (i, 0, j, 0, 0))
    return pl.pallas_call(
        _proj_kernel,
        out_shape=out_shape,
        grid=grid,
        in_specs=[tok(d),
                  pl.BlockSpec((bb, 1, 3 * d), lambda i, j: (i, 0, 0)),
                  pl.BlockSpec((1, 1, d), lambda i, j: (0, 0, 0)),
                  pl.BlockSpec((ts, LANES), lambda i, j: (j, 0)),
                  pl.BlockSpec((ts, LANES), lambda i, j: (j, 0)),
                  pl.BlockSpec(w16.shape, lambda i, j: (0, 0), pipeline_mode=pl.Buffered(1))],
        out_specs=out_specs,
        compiler_params=_params(2),
        name="proj",
    )(x, mod, norm_g.reshape(1, 1, d), cos, sin, w16)


def _diff_lambda(lam_ref, lam_init):
    lv = lam_ref[...]
    a = jnp.sum(lv[0:1] * lv[1:2], axis=1, keepdims=True)
    b = jnp.sum(lv[2:3] * lv[3:4], axis=1, keepdims=True)
    return jnp.exp(a) - jnp.exp(b) + lam_init


def _diff_finish(o0, o1, lam, g, lam_init):
    o = o0 - lam * o1
    on = o * lax.rsqrt(jnp.mean(o * o, axis=1, keepdims=True) + NORM_EPS) * g
    return on * (1.0 - lam_init)


def _diff_kernel(*refs, tile, lam_init):
    qi = pl.program_id(1)

    def one_head(h, carry):
        _diff_head(h, qi, *refs, tile=tile, lam_init=lam_init)
        return carry

    lax.fori_loop(0, N_HEADS_A, one_head, 0)


def _diff_head(h, qi, lam_ref, q_ref, k_ref, vt_ref, g_ref, o_ref, sa_ref, sb_ref, m_ref, l_ref, acc_ref,
               *, tile, lam_init):
    qc = _lane_halves(q_ref[0, h])
    m_ref[...] = jnp.full(m_ref.shape, NEG, F32)
    l_ref[...] = jnp.zeros(l_ref.shape, F32)
    acc_ref[...] = jnp.zeros(acc_ref.shape, F32)

    def produce(kb, s_ref):
        k = k_ref[0, h, pl.ds(pl.multiple_of(kb * tile, tile), tile), :]
        for c in range(2):
            s_ref[c] = lax.dot_general(k, qc[c], _NT, preferred_element_type=F32)

    def consume(kb, s_ref, mask):
        vt = vt_ref[0, h, kb]
        for c in range(2):
            s = s_ref[c]
            if mask is not None:
                s = jnp.where(mask, s, NEG)
            m_prev = m_ref[c]
            m_next = jnp.maximum(m_prev, jnp.max(s, axis=0, keepdims=True))
            alpha = jnp.exp2(m_prev - m_next)
            p = jnp.exp2(s - m_next)
            l_ref[c] = alpha * l_ref[c] + jnp.sum(p, axis=0, keepdims=True)
            acc_ref[c] = alpha * acc_ref[c] + jnp.dot(vt, p.astype(BF16), preferred_element_type=F32)
            m_ref[c] = m_next

    def two_blocks(i, carry):
        kb = 2 * i
        produce(kb + 1, sb_ref)
        consume(kb, sa_ref, None)
        produce(kb + 2, sa_ref)
        consume(kb + 1, sb_ref, None)
        return carry

    produce(0, sa_ref)
    lax.fori_loop(0, qi // 2, two_blocks, 0)
    key = lax.broadcasted_iota(jnp.int32, (tile, tile), 0)
    query = lax.broadcasted_iota(jnp.int32, (tile, tile), 1)
    diagonal = (key // CHUNK) <= (query // CHUNK)

    @pl.when(qi % 2 == 1)
    def _():
        produce(qi, sb_ref)
        consume(qi - 1, sa_ref, None)
        consume(qi, sb_ref, diagonal)

    @pl.when(qi % 2 == 0)
    def _():
        consume(qi, sa_ref, diagonal)

    lam = _diff_lambda(lam_ref, lam_init)
    o = acc_ref[0] / l_ref[0] - lam * (acc_ref[1] / l_ref[1])
    on = o * lax.rsqrt(jnp.mean(o * o, axis=0, keepdims=True) + NORM_EPS) * g_ref[...]
    o_ref[0, h] = (on * (1.0 - lam_init)).T.astype(o_ref.dtype)


def _diff_attn(lamv, q, k, vt, subln_g, lam_init):
    b, heads, s, _ = q.shape
    t = DIFF_TILE
    q_spec = pl.BlockSpec((1, heads, t, LANES), lambda bi, qi: (bi, 0, qi, 0))
    return pl.pallas_call(
        functools.partial(_diff_kernel, tile=t, lam_init=lam_init),
        out_shape=jax.ShapeDtypeStruct(q.shape, BF16),
        grid=(b, s // t),
        in_specs=[pl.BlockSpec(lamv.shape, lambda bi, qi: (0, 0)),
                  q_spec,
                  pl.BlockSpec((1, heads, s, LANES), lambda bi, qi: (bi, 0, 0, 0)),
                  pl.BlockSpec((1, heads, s // t, LANES, t), lambda bi, qi: (bi, 0, 0, 0, 0)),
                  pl.BlockSpec((LANES, 1), lambda bi, qi: (0, 0))],
        out_specs=q_spec,
        scratch_shapes=[pltpu.VMEM((2, t, t), F32)] * 2
        + [pltpu.VMEM((2, 1, t), F32)] * 2 + [pltpu.VMEM((2, LANES, t), F32)],
        compiler_params=_params(2),
        name="diff_attn",
    )(lamv, q, k, vt, subln_g.reshape(LANES, 1))


def _sb_block(z, u, cs, mask):
    sp = _softplus2(z)
    if mask is not None:
        sp = jnp.where(mask, sp, 0.0)
    rex = jnp.dot(sp.astype(BF16), u, preferred_element_type=F32)
    a = jnp.exp2((z - sp) - rex - cs)
    if mask is not None:
        a = jnp.where(mask, a, 0.0)
    return a, jnp.sum(sp, axis=1, keepdims=True)


def _sb_kernel(*refs, tile):
    qi = pl.program_id(1)

    def one_pair(hp, carry):
        _sb_head_pair(hp, qi, *refs, tile=tile)
        return carry

    lax.fori_loop(0, WIDTH_B // LANES, one_pair, 0)


def _sb_head_pair(hp, qi, q_ref, k_ref, v_ref, u_ref, o_ref, cs_ref, acc_ref, *, tile):
    qe = _lane_halves(q_ref[0, hp])
    lane = lax.broadcasted_iota(jnp.int32, (tile, LANES), 1)

    def load(kb):
        start = pl.multiple_of(kb * tile, tile)
        return k_ref[0, hp, pl.ds(start, tile), :], v_ref[0, hp, pl.ds(start, tile), :]

    def block(e, k, v, cs, mask):
        z = lax.dot_general(qe[e], k, _NT, preferred_element_type=F32)
        a, sp_sum = _sb_block(z, u_ref[...], jnp.tile(cs, (1, tile // LANES)), mask)
        return jnp.dot(a.astype(BF16), v, preferred_element_type=F32), sp_sum

    row = lax.broadcasted_iota(jnp.int32, (tile, tile), 0)
    col = lax.broadcasted_iota(jnp.int32, (tile, tile), 1)
    k_d, v_d = load(qi)
    k_p, v_p = load(jnp.maximum(qi - 1, 0))
    has_prev = qi > 0
    v_p = jnp.where(has_prev, v_p, jnp.zeros_like(v_p))
    pv = []
    for e in range(2):
        pv_d, sum_d = block(e, k_d, v_d, jnp.zeros((tile, LANES), F32), col < row)
        sum_d = jnp.broadcast_to(sum_d, (tile, LANES))
        pv_p, sum_p = block(e, k_p, v_p, sum_d, None)
        pv.append(pv_d + pv_p)
        cs_ref[e] = sum_d + jnp.where(has_prev, sum_p, 0.0)
    acc_ref[...] = jnp.where(lane < HALF, pv[0], pv[1])

    def least_decay():
        return jnp.min(jnp.minimum(cs_ref[0], cs_ref[1]))

    def more(carry):
        kb, decay = carry
        return jnp.logical_and(kb >= 0, decay < SB_DONE)

    def full_block(carry):
        kb, _ = carry
        k, v = load(kb)
        pv = []
        for e in range(2):
            cs = cs_ref[e]
            pv_e, sp_sum = block(e, k, v, cs, None)
            pv.append(pv_e)
            cs_ref[e] = cs + sp_sum
        acc_ref[...] += jnp.where(lane < HALF, pv[0], pv[1])
        return kb - 1, least_decay()

    lax.while_loop(more, full_block, (qi - 2, least_decay()))
    o_ref[0, hp] = acc_ref[...].astype(o_ref.dtype)


def _sb_attn(q, k, v):
    b, pairs, s, _ = q.shape
    t = SB_TILE
    kv_spec = pl.BlockSpec((1, pairs, s, LANES), lambda bi, qi: (bi, 0, 0, 0))
    q_spec = pl.BlockSpec((1, pairs, t, LANES), lambda bi, qi: (bi, 0, qi, 0))
    return pl.pallas_call(
        functools.partial(_sb_kernel, tile=t),
        out_shape=jax.ShapeDtypeStruct(q.shape, BF16),
        grid=(b, s // t),
        in_specs=[q_spec, kv_spec, kv_spec, pl.BlockSpec((t, t), lambda bi, qi: (0, 0))],
        out_specs=q_spec,
        scratch_shapes=[pltpu.VMEM((2, t, LANES), F32), pltpu.VMEM((t, LANES), F32)],
        compiler_params=_params(2),
        name="sb_attn",
    )(q, k, v, _suffix_matrix(t))


def _diff_sample_kernel(lam_ref, q_ref, kn_ref, vn_ref, ck_ref, cv_ref, g_ref, o_ref, *, past, lam_init):
    ns = q_ref.shape[1]
    lam = _diff_lambda(lam_ref, lam_init)
    qpos = past + lax.broadcasted_iota(jnp.int32, (2 * ns, 1), 0) % ns
    cache_ok = (lax.broadcasted_iota(jnp.int32, (2 * ns, past), 1) // CHUNK) <= (qpos // CHUNK)
    new_ok = ((past + lax.broadcasted_iota(jnp.int32, (2 * ns, ns), 1)) // CHUNK) <= (qpos // CHUNK)
    for h in range(N_HEADS_A):
        hs = slice(h * LANES, (h + 1) * LANES)
        qq = jnp.concatenate(_lane_halves(q_ref[0, :, hs]), axis=0)
        kc = ck_ref[0, :, hs].astype(BF16)
        vc = cv_ref[0, :, hs].astype(BF16)
        s_c = jnp.where(cache_ok, lax.dot_general(qq, kc, _NT, preferred_element_type=F32), NEG)
        s_n = jnp.where(new_ok, lax.dot_general(qq, kn_ref[0, :, hs], _NT, preferred_element_type=F32), NEG)
        m = jnp.maximum(jnp.max(s_c, axis=1, keepdims=True), jnp.max(s_n, axis=1, keepdims=True))
        p_c = jnp.exp2(s_c - m)
        p_n = jnp.exp2(s_n - m)
        l = jnp.sum(p_c, axis=1, keepdims=True) + jnp.sum(p_n, axis=1, keepdims=True)
        acc = (jnp.dot(p_c.astype(BF16), vc, preferred_element_type=F32)
               + jnp.dot(p_n.astype(BF16), vn_ref[0, :, hs], preferred_element_type=F32))
        o = acc / l
        o_ref[0, :, hs] = _diff_finish(o[:ns], o[ns:], lam, g_ref[...], lam_init).astype(o_ref.dtype)


def _diff_sample(lamv, q, kn, vn, ck, cv, subln_g, lam_init):
    b, ns, _ = q.shape
    past = ck.shape[1]
    new = pl.BlockSpec((1, ns, WIDTH_A), lambda i: (i, 0, 0))
    cache = pl.BlockSpec((1, past, WIDTH_A), lambda i: (i, 0, 0))
    return pl.pallas_call(
        functools.partial(_diff_sample_kernel, past=past, lam_init=lam_init),
        out_shape=jax.ShapeDtypeStruct((b, ns, WIDTH_A), BF16),
        grid=(b,),
        in_specs=[pl.BlockSpec(lamv.shape, lambda i: (0, 0)), new, new, new, cache, cache,
                  pl.BlockSpec((1, LANES), lambda i: (0, 0))],
        out_specs=new,
        compiler_params=_params(1),
        name="diff_sample",
    )(lamv, q, kn, vn, ck, cv, subln_g.reshape(1, LANES))


def _sb_sample_kernel(q_ref, kn_ref, vn_ref, ck_ref, cv_ref, un_ref, uc_ref, o_ref, *, past):
    ns = q_ref.shape[1]
    blk = CACHE_BLOCK
    t = lax.broadcasted_iota(jnp.int32, (2 * ns, ns), 0) % ns
    new_ok = lax.broadcasted_iota(jnp.int32, (2 * ns, ns), 1) < t
    lane = lax.broadcasted_iota(jnp.int32, (ns, LANES), 1)
    for hp in range(WIDTH_B // LANES):
        hs = slice(hp * LANES, (hp + 1) * LANES)
        qq = jnp.concatenate(_lane_halves(q_ref[0, :, hs]), axis=0)
        z_n = lax.dot_general(qq, kn_ref[0, :, hs], _NT, preferred_element_type=F32)
        a, cs = _sb_block(z_n, un_ref[...], 0.0, new_ok)
        acc = jnp.dot(a.astype(BF16), vn_ref[0, :, hs], preferred_element_type=F32)
        for kb in reversed(range(past // blk)):
            ks = slice(kb * blk, (kb + 1) * blk)
            z = lax.dot_general(qq, ck_ref[0, ks, hs].astype(BF16), _NT, preferred_element_type=F32)
            a, sp_sum = _sb_block(z, uc_ref[...], cs, None)
            acc += jnp.dot(a.astype(BF16), cv_ref[0, ks, hs].astype(BF16), preferred_element_type=F32)
            cs = cs + sp_sum
        o_ref[0, :, hs] = jnp.where(lane < HALF, acc[:ns], acc[ns:]).astype(o_ref.dtype)


def _sb_sample(q, kn, vn, ck, cv):
    b, ns, _ = q.shape
    past = ck.shape[1]
    new = pl.BlockSpec((1, ns, WIDTH_B), lambda i: (i, 0, 0))
    cache = pl.BlockSpec((1, past, WIDTH_B), lambda i: (i, 0, 0))
    const = lambda n: pl.BlockSpec((n, n), lambda i: (0, 0))
    return pl.pallas_call(
        functools.partial(_sb_sample_kernel, past=past),
        out_shape=jax.ShapeDtypeStruct((b, ns, WIDTH_B), BF16),
        grid=(b,),
        in_specs=[new, new, new, cache, cache, const(ns), const(CACHE_BLOCK)],
        out_specs=new,
        compiler_params=_params(1),
        name="sb_sample",
    )(q, kn, vn, ck, cv, _suffix_matrix(ns), _suffix_matrix(CACHE_BLOCK))


def _merge_kernel(x_ref, mod_ref, oa_ref, ob_ref, sza_ref, szb_ref, sga_ref, sgb_ref,
                  wa_ref, wb_ref, wo_ref, fg_ref, y_ref, *, final_norm):
    bb, ts, d = x_ref.shape
    n = bb * ts

    def flat(ref):
        if len(ref.shape) == 4:
            return jnp.concatenate([ref[0, i] for i in range(ref.shape[1])], axis=1)
        return ref[...].reshape(n, ref.shape[-1])

    ya =jnp.dot(flat(oa_ref) * flat(sza_ref), wa_ref[...], preferred_element_type=F32)
    yb = jnp.dot(flat(ob_ref) * flat(szb_ref), wb_ref[...], preferred_element_type=F32)
    mixed = flat(sga_ref).astype(F32) * ya + flat(sgb_ref).astype(F32) * yb
    out = jnp.dot(mixed.astype(BF16), wo_ref[...], preferred_element_type=F32)
    xn = x_ref[...] + mod_ref[...][:, :, 2 * d:] * out.reshape(bb, ts, d)
    if final_norm:
        xn = xn * lax.rsqrt(jnp.mean(xn * xn, axis=-1, keepdims=True) + NORM_EPS) * fg_ref[...]
    y_ref[...] = xn


def _merge(x, mod, oa, ob, sza, szb, sga, sgb, wa16, wb16, wo16, final_g, bb, ts, final_norm):
    b, s, d = x.shape
    tok = lambda width: pl.BlockSpec((bb, ts, width), lambda i, j: (i, j, 0))
    const = lambda arr: pl.BlockSpec(arr.shape, lambda i, j: (0,) * arr.ndim)

    def attn_out(arr):
        if arr.ndim == 4:
            return pl.BlockSpec((1, arr.shape[1], ts, LANES), lambda i, j: (i, 0, j, 0))
        return tok(arr.shape[-1])

    fg = final_g.reshape(1, 1, d)
    return pl.pallas_call(
        functools.partial(_merge_kernel, final_norm=final_norm),
        out_shape=jax.ShapeDtypeStruct((b, s, d), F32),
        grid=(b // bb, s // ts),
        in_specs=[tok(d), pl.BlockSpec((bb, 1, 3 * d), lambda i, j: (i, 0, 0)),
                  attn_out(oa), attn_out(ob), tok(WIDTH_A), tok(WIDTH_B), tok(d), tok(d),
                  const(wa16), const(wb16), const(wo16), const(fg)],
        out_specs=tok(d),
        compiler_params=_params(2),
        name="merge",
    )(x, mod, oa, ob, sza, szb, sga, sgb, wa16, wb16, wo16, fg)


def _rope_tables(pos):
    half = HEAD_DIM_A // 2
    inv = ROPE_THETA ** (-jnp.arange(half, dtype=F32) / half)
    ang = pos.astype(F32)[:, None] * inv[None, :]
    cos, sin = jnp.cos(ang), jnp.sin(ang)
    reps = LANES // HEAD_DIM_A
    return (jnp.tile(jnp.concatenate([cos, cos], axis=1), (1, reps)),
            jnp.tile(jnp.concatenate([-sin, sin], axis=1), (1, reps)))


def kernel(x_prompt, x_sample, c_prompt, c_sample, cache_a_k, cache_a_v, cache_b_k, cache_b_v, norm_g, w_ada, b_ada, w_in, lambda_q1, lambda_k1, lambda_q2, lambda_k2, subln_g, w_branch_a, w_branch_b, w_out, final_g):
    depth = w_in.shape[0]
    bp, sp, d = x_prompt.shape
    bs, ns, _ = x_sample.shape
    past = cache_a_k.shape[2]
    assert sp % PROJ_TOKENS == 0 and PROJ_TOKENS == DIFF_TILE and sp % SB_TILE == 0 and past % CACHE_BLOCK == 0
    assert ns % 16 == 0 and subln_g.shape[1] == LANES

    cos_p, sin_p = _rope_tables(jnp.arange(sp, dtype=jnp.int32))
    cos_s, sin_s = _rope_tables(past + jnp.arange(ns, dtype=jnp.int32))
    c_all = jnp.concatenate([c_prompt, c_sample], axis=0)

    xp, xs = x_prompt, x_sample
    caches = [[], [], [], [], [], [], [], []]
    for l in range(depth):
        lam_init = 0.8 - 0.6 * math.exp(-0.3 * l)
        final = l == depth - 1
        mod = _ada(c_all, w_ada[l], b_ada[l])
        mod_p = mod[:bp].reshape(bp, 1, 3 * d)
        mod_s = mod[bp:].reshape(bs, 1, 3 * d)
        w16 = w_in[l].astype(BF16)
        wa16, wb16, wo16 = (w[l].astype(BF16) for w in (w_branch_a, w_branch_b, w_out))
        lamv = jnp.stack([lambda_q1[l], lambda_k1[l], lambda_q2[l], lambda_k2[l]])

        (ka32, va32, kb32, vb32, qa, ka, va, qb, kb, vb, sza, szb, sga, sgb) = _proj(
            xp, mod_p, norm_g[l], cos_p, sin_p, w16, 1, PROJ_TOKENS, True)
        oa = _diff_attn(lamv, qa, ka, va, subln_g[l], lam_init)
        ob = _sb_attn(qb, kb, vb)
        xp = _merge(xp, mod_p, oa, ob, sza, szb, sga, sgb, wa16, wb16, wo16, final_g, 1, PROJ_TOKENS, final)
        for dst, val in zip(caches[:4], (ka32, va32, kb32, vb32)):
            dst.append(val)

        (ka32, va32, kb32, vb32, qa, ka, va, qb, kb, vb, sza, szb, sga, sgb) = _proj(
            xs, mod_s, norm_g[l], cos_s, sin_s, w16, bs, ns, False)
        oa = _diff_sample(lamv, qa, ka, va, cache_a_k[l].reshape(bs, past, WIDTH_A),
                          cache_a_v[l].reshape(bs, past, WIDTH_A), subln_g[l], lam_init)
        ob = _sb_sample(qb, kb, vb, cache_b_k[l].reshape(bs, past, WIDTH_B),
                        cache_b_v[l].reshape(bs, past, WIDTH_B))
        xs = _merge(xs, mod_s, oa, ob, sza, szb, sga, sgb, wa16, wb16, wo16, final_g, bs, ns, final)
        for dst, val in zip(caches[4:], (ka32, va32, kb32, vb32)):
            dst.append(val)

    def stacked(vals, shape_tail):
        arr = vals[0][None] if depth == 1 else jnp.stack(vals)
        return arr.reshape(arr.shape[:3] + shape_tail)

    tails = [(N_HEADS_A, 2, HEAD_DIM_A), (N_HEADS_A, 2 * HEAD_DIM_A),
             (N_HEADS_B, HEAD_DIM_B), (N_HEADS_B, HEAD_DIM_B)] * 2
    return (xp, xs) + tuple(stacked(v, t) for v, t in zip(caches, tails))
```

```python
import functools
import math

import jax
import jax.numpy as jnp
from jax import lax
from jax.experimental import pallas as pl
from jax.experimental.pallas import tpu as pltpu

F32 = jnp.float32
BF16 = jnp.bfloat16

CHUNK = 64
N_HEADS_A = 4
HEAD_DIM_A = 64
WIDTH_A = N_HEADS_A * 2 * HEAD_DIM_A
N_HEADS_B = 8
HEAD_DIM_B = 64
WIDTH_B = N_HEADS_B * HEAD_DIM_B
ROPE_THETA = 10000.0
NORM_EPS = 1e-6

LANES = 128
HALF = 64
VMEM_LIMIT = 56 * 1024 * 1024
NEG = -1e30
SB_DONE = 150.0

PROJ_TOKENS = 512
DIFF_TILE = 512
SB_TILE = 256
LOG2E = math.log2(math.e)
CACHE_BLOCK = 256

_NT = (((1,), (1,)), ((), ()))


def _params(n_axes):
    return pltpu.CompilerParams(dimension_semantics=("parallel",) * n_axes,
                                vmem_limit_bytes=VMEM_LIMIT)


def _lane_halves(x):
    lane = lax.broadcasted_iota(jnp.int32, x.shape, x.ndim - 1)
    zero = jnp.zeros_like(x)
    return jnp.where(lane < HALF, x, zero), jnp.where(lane >= HALF, x, zero)


def _softplus2(z):
    return jnp.maximum(z, 0.0) + jnp.log2(1.0 + jnp.exp2(jnp.minimum(z, -z)))


def _suffix_matrix(n):
    j = lax.broadcasted_iota(jnp.int32, (n, n), 0)
    s = lax.broadcasted_iota(jnp.int32, (n, n), 1)
    return (j > s).astype(BF16)


def _ada_kernel(c_ref, w_ref, b_ref, o_ref):
    o_ref[...] = jnp.dot(c_ref[...], w_ref[...], preferred_element_type=F32,
                         precision=lax.Precision.HIGHEST) + b_ref[...]


def _ada(c, w, b):
    n, d = c.shape
    width = w.shape[1]
    tn = 512
    return pl.pallas_call(
        _ada_kernel,
        out_shape=jax.ShapeDtypeStruct((n, width), F32),
        grid=(width // tn,),
        in_specs=[pl.BlockSpec((n, d), lambda j: (0, 0)),
                  pl.BlockSpec((d, tn), lambda j: (0, j)),
                  pl.BlockSpec((1, tn), lambda j: (0, j))],
        out_specs=pl.BlockSpec((n, tn), lambda j: (0, j)),
        compiler_params=_params(1),
        name="ada",
    )(c, w, b.reshape(1, width))


def _rope(x, cos, sin_signed):
    lane = lax.broadcasted_iota(jnp.int32, x.shape, 1)
    first_half = (lane % HALF) < (HALF // 2)
    partner = jnp.where(first_half, pltpu.roll(x, LANES - HALF // 2, 1), pltpu.roll(x, HALF // 2, 1))
    return x * cos + partner * sin_signed


def _proj_kernel(x_ref, mod_ref, g_ref, cos_ref, sin_ref, w_ref,
                 ka32, va32, kb32, vb32, qa16, ka16, va16, qb16, kb16, vb16, sza, szb, sga, sgb):
    bb, ts, d = x_ref.shape
    n = bb * ts
    x = x_ref[...]
    y = x * lax.rsqrt(jnp.mean(x * x, axis=-1, keepdims=True) + NORM_EPS) * g_ref[...]
    mod = mod_ref[...]
    h = (y * (1.0 + mod[:, :, d:2 * d]) + mod[:, :, :d]).reshape(n, d).astype(BF16)
    cos = jnp.broadcast_to(cos_ref[...][None], (bb, ts, LANES)).reshape(n, LANES)
    sin = jnp.broadcast_to(sin_ref[...][None], (bb, ts, LANES)).reshape(n, LANES)

    def cols(start, width):
        return jnp.dot(h, w_ref[:, start:start + width], preferred_element_type=F32)

    def put(ref, val):
        if len(ref.shape) == 3:
            ref[...] = val.reshape(ref.shape).astype(ref.dtype)
            return
        src = val.T if len(ref.shape) == 5 else val
        for i in range(ref.shape[1]):
            if len(ref.shape) == 5:
                ref[0, i, 0] = src[i * LANES:(i + 1) * LANES].astype(ref.dtype)
            else:
                ref[0, i] = src[:, i * LANES:(i + 1) * LANES].astype(ref.dtype)

    def rope_all(val):
        return jnp.concatenate(
            [_rope(val[:, i * LANES:(i + 1) * LANES], cos, sin) for i in range(WIDTH_A // LANES)], axis=1)

    wa, wb = WIDTH_A, WIDTH_B
    put(qa16, rope_all(cols(0, wa)) * (HEAD_DIM_A ** -0.5 * LOG2E))
    ka = rope_all(cols(wa, wa))
    put(ka32, ka)
    put(ka16, ka)
    va = cols(2 * wa, wa)
    if len(va32.shape) == 4:
        for i in range(N_HEADS_A):
            va32[0, :, i, :] = va[:, i * LANES:(i + 1) * LANES]
    else:
        put(va32, va)
    put(va16, va)
    za = cols(3 * wa, wa)
    put(sza, za * jax.nn.sigmoid(za))
    off = 4 * wa
    put(qb16, cols(off, wb) * (HEAD_DIM_B ** -0.5 * LOG2E))
    kb = cols(off + wb, wb)
    put(kb32, kb)
    put(kb16, kb)
    vb = cols(off + 2 * wb, wb)
    put(vb32, vb)
    put(vb16, vb)
    zb = cols(off + 3 * wb, wb)
    put(szb, zb * jax.nn.sigmoid(zb))
    off += 4 * wb
    put(sga, jax.nn.sigmoid(cols(off, d)))
    put(sgb, jax.nn.sigmoid(cols(off + d, d)))


def _proj(x, mod, norm_g, cos, sin, w16, bb, ts, head_major):
    b, s, d = x.shape
    grid = (b // bb, s // ts)
    tok = lambda width: pl.BlockSpec((bb, ts, width), lambda i, j: (i, j, 0))
    shape = lambda width, dt: jax.ShapeDtypeStruct((b, s, width), dt)
    widths32 = [WIDTH_A, WIDTH_A, WIDTH_B, WIDTH_B]
    widths16 = [WIDTH_A] * 3 + [WIDTH_B] * 3 + [WIDTH_A, WIDTH_B, d, d]
    out_shape = [shape(w, F32) for w in widths32] + [shape(w, BF16) for w in widths16]
    out_specs = [tok(w) for w in widths32 + widths16]
    if head_major:
        assert bb == 1
        for o in range(4, 10):
            groups = widths16[o - 4] // LANES
            out_shape[o] = jax.ShapeDtypeStruct((b, groups, s, LANES), BF16)
            out_specs[o] = pl.BlockSpec((1, groups, ts, LANES), lambda i, j: (i, 0, j, 0))
        out_shape[1] = jax.ShapeDtypeStruct((b, s, N_HEADS_A, LANES), F32)
        out_specs[1] = pl.BlockSpec((1, ts, N_HEADS_A, LANES), lambda i, j: (i, j, 0, 0))
        out_shape[6] = jax.ShapeDtypeStruct((b, N_HEADS_A, s // ts, LANES, ts), BF16)
        out_specs[6] = pl.BlockSpec((1, N_HEADS_A, 1, LANES, ts), lambda i, j: (i, 0, j, 0, 0))
    return pl.pallas_call(
        _proj_kernel,
        out_shape=out_shape,
        grid=grid,
        in_specs=[tok(d),
                  pl.BlockSpec((bb, 1, 3 * d), lambda i, j: (i, 0, 0)),
                  pl.BlockSpec((1, 1, d), lambda i, j: (0, 0, 0)),
                  pl.BlockSpec((ts, LANES), lambda i, j: (j, 0)),
                  pl.BlockSpec((ts, LANES), lambda i, j: (j, 0)),
                  pl.BlockSpec(w16.shape, lambda i, j: (0, 0), pipeline_mode=pl.Buffered(1))],
        out_specs=out_specs,
        compiler_params=_params(2),
        name="proj",
    )(x, mod, norm_g.reshape(1, 1, d), cos, sin, w16)


def _diff_lambda(lam_ref, lam_init):
    lv = lam_ref[...]
    a = jnp.sum(lv[0:1] * lv[1:2], axis=1, keepdims=True)
    b = jnp.sum(lv[2:3] * lv[3:4], axis=1, keepdims=True)
    return jnp.exp(a) - jnp.exp(b) + lam_init


def _diff_finish(o0, o1, lam, g, lam_init):
    o = o0 - lam * o1
    on = o * lax.rsqrt(jnp.mean(o * o, axis=1, keepdims=True) + NORM_EPS) * g
    return on * (1.0 - lam_init)


def _diff_kernel(*refs, tile, lam_init):
    qi = pl.program_id(1)

    def one_head(h, carry):
        _diff_head(h, qi, *refs, tile=tile, lam_init=lam_init)
        return carry

    lax.fori_loop(0, N_HEADS_A, one_head, 0)


def _diff_head(h, qi, lam_ref, q_ref, k_ref, vt_ref, g_ref, o_ref,
               sa_ref, sb_ref, ta_ref, tb_ref, m_ref, l_ref, acc_ref, *, tile, lam_init):
    qc = _lane_halves(q_ref[0, h])
    buf_a, buf_b = (sa_ref, ta_ref), (sb_ref, tb_ref)
    m_ref[...] = jnp.full(m_ref.shape, NEG, F32)
    l_ref[...] = jnp.zeros(l_ref.shape, F32)
    acc_ref[...] = jnp.zeros(acc_ref.shape, F32)

    def produce(kb, buf):
        s_ref, top_ref = buf
        k = k_ref[0, h, pl.ds(pl.multiple_of(kb * tile, tile), tile), :]
        for c in range(2):
            s = lax.dot_general(k, qc[c], _NT, preferred_element_type=F32)
            s_ref[c] = s
            top_ref[c] = jnp.max(s, axis=0, keepdims=True)

    def consume(kb, buf, mask):
        s_ref, top_ref = buf
        vt = vt_ref[0, h, kb]
        for c in range(2):
            s = s_ref[c]
            if mask is None:
                top = top_ref[c]
            else:
                s = jnp.where(mask, s, NEG)
                top = jnp.max(s, axis=0, keepdims=True)
            m_prev = m_ref[c]
            m_next = jnp.maximum(m_prev, top)
            alpha = jnp.exp2(m_prev - m_next)
            p = jnp.exp2(s - m_next)
            l_ref[c] = alpha * l_ref[c] + jnp.sum(p, axis=0, keepdims=True)
            acc_ref[c] = alpha * acc_ref[c] + jnp.dot(vt, p.astype(BF16), preferred_element_type=F32)
            m_ref[c] = m_next

    def two_blocks(i, carry):
        kb = 2 * i
        produce(kb + 1, buf_b)
        consume(kb, buf_a, None)
        produce(kb + 2, buf_a)
        consume(kb + 1, buf_b, None)
        return carry

    produce(0, buf_a)
    lax.fori_loop(0, qi // 2, two_blocks, 0)
    key = lax.broadcasted_iota(jnp.int32, (tile, tile), 0)
    query = lax.broadcasted_iota(jnp.int32, (tile, tile), 1)
    diagonal = (key // CHUNK) <= (query // CHUNK)

    @pl.when(qi % 2 == 1)
    def _():
        produce(qi, buf_b)
        consume(qi - 1, buf_a, None)
        consume(qi, buf_b, diagonal)

    @pl.when(qi % 2 == 0)
    def _():
        consume(qi, buf_a, diagonal)

    lam = _diff_lambda(lam_ref, lam_init)
    o = acc_ref[0] / l_ref[0] - lam * (acc_ref[1] / l_ref[1])
    on = o * lax.rsqrt(jnp.mean(o * o, axis=0, keepdims=True) + NORM_EPS) * g_ref[...]
    o_ref[0, h] = (on * (1.0 - lam_init)).T.astype(o_ref.dtype)


def _diff_attn(lamv, q, k, vt, subln_g, lam_init):
    b, heads, s, _ = q.shape
    t = DIFF_TILE
    q_spec = pl.BlockSpec((1, heads, t, LANES), lambda bi, qi: (bi, 0, qi, 0))
    return pl.pallas_call(
        functools.partial(_diff_kernel, tile=t, lam_init=lam_init),
        out_shape=jax.ShapeDtypeStruct(q.shape, BF16),
        grid=(b, s // t),
        in_specs=[pl.BlockSpec(lamv.shape, lambda bi, qi: (0, 0)),
                  q_spec,
                  pl.BlockSpec((1, heads, s, LANES), lambda bi, qi: (bi, 0, 0, 0)),
                  pl.BlockSpec((1, heads, s // t, LANES, t), lambda bi, qi: (bi, 0, 0, 0, 0)),
                  pl.BlockSpec((LANES, 1), lambda bi, qi: (0, 0))],
        out_specs=q_spec,
        scratch_shapes=[pltpu.VMEM((2, t, t), F32)] * 2
        + [pltpu.VMEM((2, 1, t), F32)] * 4 + [pltpu.VMEM((2, LANES, t), F32)],
        compiler_params=_params(2),
        name="diff_attn",
    )(lamv, q, k, vt, subln_g.reshape(LANES, 1))


def _sb_block(z, u, cs, mask):
    sp = _softplus2(z)
    if mask is not None:
        sp = jnp.where(mask, sp, 0.0)
    rex = jnp.dot(sp.astype(BF16), u, preferred_element_type=F32)
    a = jnp.exp2((z - sp) - rex - cs)
    if mask is not None:
        a = jnp.where(mask, a, 0.0)
    return a, jnp.sum(sp, axis=1, keepdims=True)


def _sb_kernel(q_ref, k_ref, v_ref, u_ref, o_ref, cs_ref, acc_ref, *, tile):
    qi = pl.program_id(1)
    pairs = q_ref.shape[1]
    refs = (q_ref, k_ref, v_ref, u_ref, cs_ref, acc_ref)
    slowest = functools.reduce(jnp.minimum, [_sb_near(hp, qi, *refs, tile=tile) for hp in range(pairs)])

    @pl.when(jnp.logical_and(qi >= 2, jnp.min(slowest) < SB_DONE))
    def _():
        def one_pair(hp, carry):
            _sb_far(hp, qi, *refs, tile=tile)
            return carry

        lax.fori_loop(0, pairs, one_pair, 0)

    for hp in range(pairs):
        o_ref[0, hp] = acc_ref[hp].astype(o_ref.dtype)


def _sb_pair_ops(hp, q_ref, k_ref, v_ref, u_ref, tile):
    qe = _lane_halves(q_ref[0, hp])

    def load(kb):
        start = pl.multiple_of(kb * tile, tile)
        return k_ref[0, hp, pl.ds(start, tile), :], v_ref[0, hp, pl.ds(start, tile), :]

    def block(e, k, v, cs, mask):
        z = lax.dot_general(qe[e], k, _NT, preferred_element_type=F32)
        a, sp_sum = _sb_block(z, u_ref[...], jnp.tile(cs, (1, tile // LANES)), mask)
        return jnp.dot(a.astype(BF16), v, preferred_element_type=F32), sp_sum

    return load, block


def _sb_near(hp, qi, q_ref, k_ref, v_ref, u_ref, cs_ref, acc_ref, *, tile):
    load, block = _sb_pair_ops(hp, q_ref, k_ref, v_ref, u_ref, tile)
    lane = lax.broadcasted_iota(jnp.int32, (tile, LANES), 1)
    row = lax.broadcasted_iota(jnp.int32, (tile, tile), 0)
    col = lax.broadcasted_iota(jnp.int32, (tile, tile), 1)
    k_d, v_d = load(qi)
    k_p, v_p = load(jnp.maximum(qi - 1, 0))
    has_prev = qi > 0
    v_p = jnp.where(has_prev, v_p, jnp.zeros_like(v_p))
    pv, sums = [], []
    for e in range(2):
        pv_d, sum_d = block(e, k_d, v_d, jnp.zeros((tile, LANES), F32), col < row)
        sum_d = jnp.broadcast_to(sum_d, (tile, LANES))
        pv_p, sum_p = block(e, k_p, v_p, sum_d, None)
        pv.append(pv_d + pv_p)
        sums.append(sum_d + jnp.where(has_prev, sum_p, 0.0))
        cs_ref[hp, e] = sums[e]
    acc_ref[hp] = jnp.where(lane < HALF, pv[0], pv[1])
    return jnp.minimum(sums[0], sums[1])


def _sb_far(hp, qi, q_ref, k_ref, v_ref, u_ref, cs_ref, acc_ref, *, tile):
    load, block = _sb_pair_ops(hp, q_ref, k_ref, v_ref, u_ref, tile)
    lane = lax.broadcasted_iota(jnp.int32, (tile, LANES), 1)

    def least_decay():
        return jnp.min(jnp.minimum(cs_ref[hp, 0], cs_ref[hp, 1]))

    def more(carry):
        kb, decay = carry
        return jnp.logical_and(kb >= 0, decay < SB_DONE)

    def full_block(carry):
        kb, _ = carry
        k, v = load(kb)
        pv = []
        for e in range(2):
            cs = cs_ref[hp, e]
            pv_e, sp_sum = block(e, k, v, cs, None)
            pv.append(pv_e)
            cs_ref[hp, e] = cs + sp_sum
        acc_ref[hp] += jnp.where(lane < HALF, pv[0], pv[1])
        return kb - 1, least_decay()

    lax.while_loop(more, full_block, (qi - 2, least_decay()))


def _sb_attn(q, k, v):
    b, pairs, s, _ = q.shape
    t = SB_TILE
    kv_spec = pl.BlockSpec((1, pairs, s, LANES), lambda bi, qi: (bi, 0, 0, 0))
    q_spec = pl.BlockSpec((1, pairs, t, LANES), lambda bi, qi: (bi, 0, qi, 0))
    return pl.pallas_call(
        functools.partial(_sb_kernel, tile=t),
        out_shape=jax.ShapeDtypeStruct(q.shape, BF16),
        grid=(b, s // t),
        in_specs=[q_spec, kv_spec, kv_spec, pl.BlockSpec((t, t), lambda bi, qi: (0, 0))],
        out_specs=q_spec,
        scratch_shapes=[pltpu.VMEM((pairs, 2, t, LANES), F32), pltpu.VMEM((pairs, t, LANES), F32)],
        compiler_params=_params(2),
        name="sb_attn",
    )(q, k, v, _suffix_matrix(t))


def _diff_sample_kernel(lam_ref, q_ref, kn_ref, vn_ref, ck_ref, cv_ref, g_ref, o_ref, *, past, lam_init):
    ns = q_ref.shape[1]
    lam = _diff_lambda(lam_ref, lam_init)
    qpos = past + lax.broadcasted_iota(jnp.int32, (2 * ns, 1), 0) % ns
    cache_ok = (lax.broadcasted_iota(jnp.int32, (2 * ns, past), 1) // CHUNK) <= (qpos // CHUNK)
    new_ok = ((past + lax.broadcasted_iota(jnp.int32, (2 * ns, ns), 1)) // CHUNK) <= (qpos // CHUNK)
    for h in range(N_HEADS_A):
        hs = slice(h * LANES, (h + 1) * LANES)
        qq = jnp.concatenate(_lane_halves(q_ref[0, :, hs]), axis=0)
        kc = ck_ref[0, :, hs].astype(BF16)
        vc = cv_ref[0, :, h, :].astype(BF16)
        s_c = jnp.where(cache_ok, lax.dot_general(qq, kc, _NT, preferred_element_type=F32), NEG)
        s_n = jnp.where(new_ok, lax.dot_general(qq, kn_ref[0, :, hs], _NT, preferred_element_type=F32), NEG)
        m = jnp.maximum(jnp.max(s_c, axis=1, keepdims=True), jnp.max(s_n, axis=1, keepdims=True))
        p_c = jnp.exp2(s_c - m)
        p_n = jnp.exp2(s_n - m)
        l = jnp.sum(p_c, axis=1, keepdims=True) + jnp.sum(p_n, axis=1, keepdims=True)
        acc = (jnp.dot(p_c.astype(BF16), vc, preferred_element_type=F32)
               + jnp.dot(p_n.astype(BF16), vn_ref[0, :, hs], preferred_element_type=F32))
        o = acc / l
        o_ref[0, :, hs] = _diff_finish(o[:ns], o[ns:], lam, g_ref[...], lam_init).astype(o_ref.dtype)


def _diff_sample(lamv, q, kn, vn, ck, cv, subln_g, lam_init):
    b, ns, _ = q.shape
    past = ck.shape[1]
    new = pl.BlockSpec((1, ns, WIDTH_A), lambda i: (i, 0, 0))
    cache = pl.BlockSpec((1, past, WIDTH_A), lambda i: (i, 0, 0))
    return pl.pallas_call(
        functools.partial(_diff_sample_kernel, past=past, lam_init=lam_init),
        out_shape=jax.ShapeDtypeStruct((b, ns, WIDTH_A), BF16),
        grid=(b,),
        in_specs=[pl.BlockSpec(lamv.shape, lambda i: (0, 0)), new, new, new, cache,
                  pl.BlockSpec((1, past, N_HEADS_A, LANES), lambda i: (i, 0, 0, 0)),
                  pl.BlockSpec((1, LANES), lambda i: (0, 0))],
        out_specs=new,
        compiler_params=_params(1),
        name="diff_sample",
    )(lamv, q, kn, vn, ck, cv, subln_g.reshape(1, LANES))


def _sb_sample_kernel(q_ref, kn_ref, vn_ref, ck_ref, cv_ref, un_ref, uc_ref, o_ref, *, past):
    ns = q_ref.shape[1]
    blk = CACHE_BLOCK
    t = lax.broadcasted_iota(jnp.int32, (2 * ns, ns), 0) % ns
    new_ok = lax.broadcasted_iota(jnp.int32, (2 * ns, ns), 1) < t
    lane = lax.broadcasted_iota(jnp.int32, (ns, LANES), 1)
    for hp in range(WIDTH_B // LANES):
        hs = slice(hp * LANES, (hp + 1) * LANES)
        qq = jnp.concatenate(_lane_halves(q_ref[0, :, hs]), axis=0)
        z_n = lax.dot_general(qq, kn_ref[0, :, hs], _NT, preferred_element_type=F32)
        a, cs = _sb_block(z_n, un_ref[...], 0.0, new_ok)
        acc = jnp.dot(a.astype(BF16), vn_ref[0, :, hs], preferred_element_type=F32)
        for kb in reversed(range(past // blk)):
            ks = slice(kb * blk, (kb + 1) * blk)
            z = lax.dot_general(qq, ck_ref[0, ks, hs].astype(BF16), _NT, preferred_element_type=F32)
            a, sp_sum = _sb_block(z, uc_ref[...], cs, None)
            acc += jnp.dot(a.astype(BF16), cv_ref[0, ks, hs].astype(BF16), preferred_element_type=F32)
            cs = cs + sp_sum
        o_ref[0, :, hs] = jnp.where(lane < HALF, acc[:ns], acc[ns:]).astype(o_ref.dtype)


def _sb_sample(q, kn, vn, ck, cv):
    b, ns, _ = q.shape
    past = ck.shape[1]
    new = pl.BlockSpec((1, ns, WIDTH_B), lambda i: (i, 0, 0))
    cache = pl.BlockSpec((1, past, WIDTH_B), lambda i: (i, 0, 0))
    const = lambda n: pl.BlockSpec((n, n), lambda i: (0, 0))
    return pl.pallas_call(
        functools.partial(_sb_sample_kernel, past=past),
        out_shape=jax.ShapeDtypeStruct((b, ns, WIDTH_B), BF16),
        grid=(b,),
        in_specs=[new, new, new, cache, cache, const(ns), const(CACHE_BLOCK)],
        out_specs=new,
        compiler_params=_params(1),
        name="sb_sample",
    )(q, kn, vn, ck, cv, _suffix_matrix(ns), _suffix_matrix(CACHE_BLOCK))


def _merge_kernel(x_ref, mod_ref, oa_ref, ob_ref, sza_ref, szb_ref, sga_ref, sgb_ref,
                  wa_ref, wb_ref, wo_ref, fg_ref, y_ref, *, final_norm):
    bb, ts, d = x_ref.shape
    n = bb * ts

    def flat(ref):
        if len(ref.shape) == 4:
            return jnp.concatenate([ref[0, i] for i in range(ref.shape[1])], axis=1)
        return ref[...].reshape(n, ref.shape[-1])

    ya =jnp.dot(flat(oa_ref) * flat(sza_ref), wa_ref[...], preferred_element_type=F32)
    yb = jnp.dot(flat(ob_ref) * flat(szb_ref), wb_ref[...], preferred_element_type=F32)
    mixed = flat(sga_ref).astype(F32) * ya + flat(sgb_ref).astype(F32) * yb
    out = jnp.dot(mixed.astype(BF16), wo_ref[...], preferred_element_type=F32)
    xn = x_ref[...] + mod_ref[...][:, :, 2 * d:] * out.reshape(bb, ts, d)
    if final_norm:
        xn = xn * lax.rsqrt(jnp.mean(xn * xn, axis=-1, keepdims=True) + NORM_EPS) * fg_ref[...]
    y_ref[...] = xn


def _merge(x, mod, oa, ob, sza, szb, sga, sgb, wa16, wb16, wo16, final_g, bb, ts, final_norm):
    b, s, d = x.shape
    tok = lambda width: pl.BlockSpec((bb, ts, width), lambda i, j: (i, j, 0))
    const = lambda arr: pl.BlockSpec(arr.shape, lambda i, j: (0,) * arr.ndim)

    def attn_out(arr):
        if arr.ndim == 4:
            return pl.BlockSpec((1, arr.shape[1], ts, LANES), lambda i, j: (i, 0, j, 0))
        return tok(arr.shape[-1])

    fg = final_g.reshape(1, 1, d)
    return pl.pallas_call(
        functools.partial(_merge_kernel, final_norm=final_norm),
        out_shape=jax.ShapeDtypeStruct((b, s, d), F32),
        grid=(b // bb, s // ts),
        in_specs=[tok(d), pl.BlockSpec((bb, 1, 3 * d), lambda i, j: (i, 0, 0)),
                  attn_out(oa), attn_out(ob), tok(WIDTH_A), tok(WIDTH_B), tok(d), tok(d),
                  const(wa16), const(wb16), const(wo16), const(fg)],
        out_specs=tok(d),
        compiler_params=_params(2),
        name="merge",
    )(x, mod, oa, ob, sza, szb, sga, sgb, wa16, wb16, wo16, fg)


def _rope_tables(pos):
    half = HEAD_DIM_A // 2
    inv = ROPE_THETA ** (-jnp.arange(half, dtype=F32) / half)
    ang = pos.astype(F32)[:, None] * inv[None, :]
    cos, sin = jnp.cos(ang), jnp.sin(ang)
    reps = LANES // HEAD_DIM_A
    return (jnp.tile(jnp.concatenate([cos, cos], axis=1), (1, reps)),
            jnp.tile(jnp.concatenate([-sin, sin], axis=1), (1, reps)))


def kernel(x_prompt, x_sample, c_prompt, c_sample, cache_a_k, cache_a_v, cache_b_k, cache_b_v, norm_g, w_ada, b_ada, w_in, lambda_q1, lambda_k1, lambda_q2, lambda_k2, subln_g, w_branch_a, w_branch_b, w_out, final_g):
    depth = w_in.shape[0]
    bp, sp, d = x_prompt.shape
    bs, ns, _ = x_sample.shape
    past = cache_a_k.shape[2]
    assert sp % PROJ_TOKENS == 0 and PROJ_TOKENS == DIFF_TILE and sp % SB_TILE == 0 and past % CACHE_BLOCK == 0
    assert ns % 16 == 0 and subln_g.shape[1] == LANES

    cos_p, sin_p = _rope_tables(jnp.arange(sp, dtype=jnp.int32))
    cos_s, sin_s = _rope_tables(past + jnp.arange(ns, dtype=jnp.int32))
    c_all = jnp.concatenate([c_prompt, c_sample], axis=0)

    xp, xs = x_prompt, x_sample
    caches = [[], [], [], [], [], [], [], []]
    for l in range(depth):
        lam_init = 0.8 - 0.6 * math.exp(-0.3 * l)
        final = l == depth - 1
        mod = _ada(c_all, w_ada[l], b_ada[l])
        mod_p = mod[:bp].reshape(bp, 1, 3 * d)
        mod_s = mod[bp:].reshape(bs, 1, 3 * d)
        w16 = w_in[l].astype(BF16)
        wa16, wb16, wo16 = (w[l].astype(BF16) for w in (w_branch_a, w_branch_b, w_out))
        lamv = jnp.stack([lambda_q1[l], lambda_k1[l], lambda_q2[l], lambda_k2[l]])

        (ka32, va32, kb32, vb32, qa, ka, va, qb, kb, vb, sza, szb, sga, sgb) = _proj(
            xp, mod_p, norm_g[l], cos_p, sin_p, w16, 1, PROJ_TOKENS, True)
        oa = _diff_attn(lamv, qa, ka, va, subln_g[l], lam_init)
        ob = _sb_attn(qb, kb, vb)
        xp = _merge(xp, mod_p, oa, ob, sza, szb, sga, sgb, wa16, wb16, wo16, final_g, 1, PROJ_TOKENS, final)
        for dst, val in zip(caches[:4], (ka32, va32, kb32, vb32)):
            dst.append(val)

        (ka32, va32, kb32, vb32, qa, ka, va, qb, kb, vb, sza, szb, sga, sgb) = _proj(
            xs, mod_s, norm_g[l], cos_s, sin_s, w16, bs, ns, False)
        oa = _diff_sample(lamv, qa, ka, va, cache_a_k[l].reshape(bs, past, WIDTH_A),
                          cache_a_v[l], subln_g[l], lam_init)
        ob = _sb_sample(qb, kb, vb, cache_b_k[l].reshape(bs, past, WIDTH_B),
                        cache_b_v[l].reshape(bs, past, WIDTH_B))
        xs = _merge(xs, mod_s, oa, ob, sza, szb, sga, sgb, wa16, wb16, wo16, final_g, bs, ns, final)
        for dst, val in zip(caches[4:], (ka32, va32, kb32, vb32)):
            dst.append(val)

    def stacked(vals, shape_tail):
        arr = vals[0][None] if depth == 1 else jnp.stack(vals)
        return arr.reshape(arr.shape[:3] + shape_tail)

    tails = [(N_HEADS_A, 2, HEAD_DIM_A), (N_HEADS_A, 2 * HEAD_DIM_A),
             (N_HEADS_B, HEAD_DIM_B), (N_HEADS_B, HEAD_DIM_B)] * 2
    return (xp, xs) + tuple(stacked(v, t) for v, t in zip(caches, tails))
```

```python
import functools
import math

import jax
import jax.numpy as jnp
from jax import lax
from jax.experimental import pallas as pl
from jax.experimental.pallas import tpu as pltpu

F32 = jnp.float32
BF16 = jnp.bfloat16

CHUNK = 64
N_HEADS_A = 4
HEAD_DIM_A = 64
WIDTH_A = N_HEADS_A * 2 * HEAD_DIM_A
N_HEADS_B = 8
HEAD_DIM_B = 64
WIDTH_B = N_HEADS_B * HEAD_DIM_B
ROPE_THETA = 10000.0
NORM_EPS = 1e-6

LANES = 128
HALF = 64
VMEM_LIMIT = 56 * 1024 * 1024
NEG = -1e30
SB_DONE = 150.0

PROJ_TOKENS = 512
DIFF_TILE = 512
SB_TILE = 256
SB_NEAR_BLOCKS = 1
LOG2E = math.log2(math.e)
CACHE_BLOCK = 256

_NT = (((1,), (1,)), ((), ()))


def _params(n_axes):
    return pltpu.CompilerParams(dimension_semantics=("parallel",) * n_axes,
                                vmem_limit_bytes=VMEM_LIMIT)


def _lane_halves(x):
    lane = lax.broadcasted_iota(jnp.int32, x.shape, x.ndim - 1)
    zero = jnp.zeros_like(x)
    return jnp.where(lane < HALF, x, zero), jnp.where(lane >= HALF, x, zero)


def _softplus2(z):
    return jnp.maximum(z, 0.0) + jnp.log2(1.0 + jnp.exp2(jnp.minimum(z, -z)))


def _suffix_matrix(n):
    j = lax.broadcasted_iota(jnp.int32, (n, n), 0)
    s = lax.broadcasted_iota(jnp.int32, (n, n), 1)
    return (j > s).astype(BF16)


def _ada_kernel(c_ref, w_ref, b_ref, o_ref):
    o_ref[...] = jnp.dot(c_ref[...], w_ref[...], preferred_element_type=F32,
                         precision=lax.Precision.HIGHEST) + b_ref[...]


def _ada(c, w, b):
    n, d = c.shape
    width = w.shape[1]
    tn = 512
    return pl.pallas_call(
        _ada_kernel,
        out_shape=jax.ShapeDtypeStruct((n, width), F32),
        grid=(width // tn,),
        in_specs=[pl.BlockSpec((n, d), lambda j: (0, 0)),
                  pl.BlockSpec((d, tn), lambda j: (0, j)),
                  pl.BlockSpec((1, tn), lambda j: (0, j))],
        out_specs=pl.BlockSpec((n, tn), lambda j: (0, j)),
        compiler_params=_params(1),
        name="ada",
    )(c, w, b.reshape(1, width))


def _rope(x, cos, sin_signed):
    lane = lax.broadcasted_iota(jnp.int32, x.shape, 1)
    first_half = (lane % HALF) < (HALF // 2)
    partner = jnp.where(first_half, pltpu.roll(x, LANES - HALF // 2, 1), pltpu.roll(x, HALF // 2, 1))
    return x * cos + partner * sin_signed


def _proj_kernel(x_ref, mod_ref, g_ref, cos_ref, sin_ref, w_ref,
                 ka32, va32, kb32, vb32, qa16, ka16, va16, qb16, kb16, vb16, sza, szb, sga, sgb):
    bb, ts, d = x_ref.shape
    n = bb * ts
    x = x_ref[...]
    y = x * lax.rsqrt(jnp.mean(x * x, axis=-1, keepdims=True) + NORM_EPS) * g_ref[...]
    mod = mod_ref[...]
    h = (y * (1.0 + mod[:, :, d:2 * d]) + mod[:, :, :d]).reshape(n, d).astype(BF16)
    cos = jnp.broadcast_to(cos_ref[...][None], (bb, ts, LANES)).reshape(n, LANES)
    sin = jnp.broadcast_to(sin_ref[...][None], (bb, ts, LANES)).reshape(n, LANES)

    def cols(start, width):
        return jnp.dot(h, w_ref[:, start:start + width], preferred_element_type=F32)

    def put(ref, val):
        if len(ref.shape) == 3:
            ref[...] = val.reshape(ref.shape).astype(ref.dtype)
            return
        src = val.T if len(ref.shape) == 5 else val
        for i in range(ref.shape[1]):
            if len(ref.shape) == 5:
                ref[0, i, 0] = src[i * LANES:(i + 1) * LANES].astype(ref.dtype)
            else:
                ref[0, i] = src[:, i * LANES:(i + 1) * LANES].astype(ref.dtype)

    def rope_all(val):
        return jnp.concatenate(
            [_rope(val[:, i * LANES:(i + 1) * LANES], cos, sin) for i in range(WIDTH_A // LANES)], axis=1)

    wa, wb = WIDTH_A, WIDTH_B
    put(qa16, rope_all(cols(0, wa)) * (HEAD_DIM_A ** -0.5 * LOG2E))
    ka = rope_all(cols(wa, wa))
    put(ka32, ka)
    put(ka16, ka)
    va = cols(2 * wa, wa)
    if len(va32.shape) == 4:
        for i in range(N_HEADS_A):
            va32[0, :, i, :] = va[:, i * LANES:(i + 1) * LANES]
    else:
        put(va32, va)
    put(va16, va)
    za = cols(3 * wa, wa)
    put(sza, za * jax.nn.sigmoid(za))
    off = 4 * wa
    put(qb16, cols(off, wb) * (HEAD_DIM_B ** -0.5 * LOG2E))
    kb = cols(off + wb, wb)
    put(kb32, kb)
    put(kb16, kb)
    vb = cols(off + 2 * wb, wb)
    put(vb32, vb)
    put(vb16, vb)
    zb = cols(off + 3 * wb, wb)
    put(szb, zb * jax.nn.sigmoid(zb))
    off += 4 * wb
    put(sga, jax.nn.sigmoid(cols(off, d)))
    put(sgb, jax.nn.sigmoid(cols(off + d, d)))


def _proj(x, mod, norm_g, cos, sin, w16, bb, ts, head_major):
    b, s, d = x.shape
    grid = (b // bb, s // ts)
    tok = lambda width: pl.BlockSpec((bb, ts, width), lambda i, j: (i, j, 0))
    shape = lambda width, dt: jax.ShapeDtypeStruct((b, s, width), dt)
    widths32 = [WIDTH_A, WIDTH_A, WIDTH_B, WIDTH_B]
    widths16 = [WIDTH_A] * 3 + [WIDTH_B] * 3 + [WIDTH_A, WIDTH_B, d, d]
    out_shape = [shape(w, F32) for w in widths32] + [shape(w, BF16) for w in widths16]
    out_specs = [tok(w) for w in widths32 + widths16]
    if head_major:
        assert bb == 1
        for o in range(4, 10):
            groups = widths16[o - 4] // LANES
            out_shape[o] = jax.ShapeDtypeStruct((b, groups, s, LANES), BF16)
            out_specs[o] = pl.BlockSpec((1, groups, ts, LANES), lambda i, j: (i, 0, j, 0))
        out_shape[1] = jax.ShapeDtypeStruct((b, s, N_HEADS_A, LANES), F32)
        out_specs[1] = pl.BlockSpec((1, ts, N_HEADS_A, LANES), lambda i, j: (i, j, 0, 0))
        out_shape[6] = jax.ShapeDtypeStruct((b, N_HEADS_A, s // ts, LANES, ts), BF16)
        out_specs[6] = pl.BlockSpec((1, N_HEADS_A, 1, LANES, ts), lambda i, j: (i, 0, j, 0, 0))
    return pl.pallas_call(
        _proj_kernel,
        out_shape=out_shape,
        grid=grid,
        in_specs=[tok(d),
                  pl.BlockSpec((bb, 1, 3 * d), lambda i, j: (i, 0, 0)),
                  pl.BlockSpec((1, 1, d), lambda i, j: (0, 0, 0)),
                  pl.BlockSpec((ts, LANES), lambda i, j: (j, 0)),
                  pl.BlockSpec((ts, LANES), lambda i, j: (j, 0)),
                  pl.BlockSpec(w16.shape, lambda i, j: (0, 0), pipeline_mode=pl.Buffered(1))],
        out_specs=out_specs,
        compiler_params=_params(2),
        name="proj",
    )(x, mod, norm_g.reshape(1, 1, d), cos, sin, w16)


def _diff_lambda(lam_ref, lam_init):
    lv = lam_ref[...]
    a = jnp.sum(lv[0:1] * lv[1:2], axis=1, keepdims=True)
    b = jnp.sum(lv[2:3] * lv[3:4], axis=1, keepdims=True)
    return jnp.exp(a) - jnp.exp(b) + lam_init


def _diff_finish(o0, o1, lam, g, lam_init):
    o = o0 - lam * o1
    on = o * lax.rsqrt(jnp.mean(o * o, axis=1, keepdims=True) + NORM_EPS) * g
    return on * (1.0 - lam_init)


def _diff_kernel(*refs, tile, lam_init):
    qi = pl.program_id(1)
    for h in range(N_HEADS_A):
        _diff_head(h, qi, *refs, tile=tile, lam_init=lam_init)


def _diff_head(h, qi, lam_ref, q_ref, k_ref, vt_ref, g_ref, o_ref,
               sa_ref, sb_ref, ta_ref, tb_ref, m_ref, l_ref, acc_ref, *, tile, lam_init):
    qc = _lane_halves(q_ref[0, h])
    buf_a, buf_b = (sa_ref, ta_ref), (sb_ref, tb_ref)
    m_ref[...] = jnp.full(m_ref.shape, NEG, F32)
    l_ref[...] = jnp.zeros(l_ref.shape, F32)
    acc_ref[...] = jnp.zeros(acc_ref.shape, F32)

    def produce(kb, buf):
        s_ref, top_ref = buf
        k = k_ref[0, h, pl.ds(pl.multiple_of(kb * tile, tile), tile), :]
        for c in range(2):
            s = lax.dot_general(k, qc[c], _NT, preferred_element_type=F32)
            s_ref[c] = s
            top_ref[c] = jnp.max(s, axis=0, keepdims=True)

    def consume(kb, buf):
        s_ref, top_ref = buf
        vt = vt_ref[0, h, kb]
        for c in range(2):
            s = s_ref[c]
            m_prev = m_ref[c]
            m_next = jnp.maximum(m_prev, top_ref[c])
            alpha = jnp.exp2(m_prev - m_next)
            p = jnp.exp2(s - m_next)
            l_ref[c] = alpha * l_ref[c] + jnp.sum(p, axis=0, keepdims=True)
            acc_ref[c] = alpha * acc_ref[c] + jnp.dot(vt, p.astype(BF16), preferred_element_type=F32)
            m_ref[c] = m_next

    def two_blocks(i, carry):
        kb = 2 * i
        produce(kb + 1, buf_b)
        consume(kb, buf_a)
        produce(kb + 2, buf_a)
        consume(kb + 1, buf_b)
        return carry

    half = tile // 2

    def produce_diagonal(buf):
        s_ref, _ = buf
        k = k_ref[0, h, pl.ds(pl.multiple_of(qi * tile, tile), tile), :]
        for c in range(2):
            s_ref[c, :half, :] = lax.dot_general(k[:half], qc[c], _NT, preferred_element_type=F32)
            s_ref[c, half:, half:] = lax.dot_general(k[half:], qc[c][half:], _NT, preferred_element_type=F32)

    def consume_diagonal(buf):
        s_ref, _ = buf
        vt = vt_ref[0, h, qi]
        key = lax.broadcasted_iota(jnp.int32, (half, tile), 0)
        query = lax.broadcasted_iota(jnp.int32, (half, tile), 1)
        visible = (key // CHUNK) <= (query // CHUNK)
        for c in range(2):
            s0 = jnp.where(visible, s_ref[c, :half, :], NEG)
            s1 = jnp.where(visible[:, :half], s_ref[c, half:, half:], NEG)
            top0 = jnp.max(s0, axis=0, keepdims=True)
            top = jnp.concatenate(
                [top0[:, :half], jnp.maximum(top0[:, half:], jnp.max(s1, axis=0, keepdims=True))], axis=1)
            m_prev = m_ref[c]
            m_next = jnp.maximum(m_prev, top)
            alpha = jnp.exp2(m_prev - m_next)
            p0 = jnp.exp2(s0 - m_next)
            p1 = jnp.exp2(s1 - m_next[:, half:])
            sum0 = jnp.sum(p0, axis=0, keepdims=True)
            l_prev = alpha * l_ref[c]
            l_ref[c, :, :half] = l_prev[:, :half] + sum0[:, :half]
            l_ref[c, :, half:] = l_prev[:, half:] + sum0[:, half:] + jnp.sum(p1, axis=0, keepdims=True)
            pv0 = jnp.dot(vt[:, :half], p0.astype(BF16), preferred_element_type=F32)
            pv1 = jnp.dot(vt[:, half:], p1.astype(BF16), preferred_element_type=F32)
            acc_prev = alpha * acc_ref[c]
            acc_ref[c, :, :half] = acc_prev[:, :half] + pv0[:, :half]
            acc_ref[c, :, half:] = acc_prev[:, half:] + pv0[:, half:] + pv1
            m_ref[c] = m_next

    produce(0, buf_a)
    lax.fori_loop(0, qi // 2, two_blocks, 0)

    @pl.when(qi % 2 == 1)
    def _():
        produce_diagonal(buf_b)
        consume(qi - 1, buf_a)
        consume_diagonal(buf_b)

    @pl.when(qi % 2 == 0)
    def _():
        consume_diagonal(buf_a)

    lam = _diff_lambda(lam_ref, lam_init)
    o = acc_ref[0] * (1.0 / l_ref[0]) - acc_ref[1] * (lam / l_ref[1])
    on = o * lax.rsqrt(jnp.mean(o * o, axis=0, keepdims=True) + NORM_EPS) * g_ref[...]
    o_ref[0, h] = (on * (1.0 - lam_init)).T.astype(o_ref.dtype)


def _diff_attn(lamv, q, k, vt, subln_g, lam_init):
    b, heads, s, _ = q.shape
    t = DIFF_TILE
    q_spec = pl.BlockSpec((1, heads, t, LANES), lambda bi, qi: (bi, 0, qi, 0))
    return pl.pallas_call(
        functools.partial(_diff_kernel, tile=t, lam_init=lam_init),
        out_shape=jax.ShapeDtypeStruct(q.shape, BF16),
        grid=(b, s // t),
        in_specs=[pl.BlockSpec(lamv.shape, lambda bi, qi: (0, 0)),
                  q_spec,
                  pl.BlockSpec((1, heads, s, LANES), lambda bi, qi: (bi, 0, 0, 0)),
                  pl.BlockSpec((1, heads, s // t, LANES, t), lambda bi, qi: (bi, 0, 0, 0, 0)),
                  pl.BlockSpec((LANES, 1), lambda bi, qi: (0, 0))],
        out_specs=q_spec,
        scratch_shapes=[pltpu.VMEM((2, t, t), F32)] * 2
        + [pltpu.VMEM((2, 1, t), F32)] * 4 + [pltpu.VMEM((2, LANES, t), F32)],
        compiler_params=_params(2),
        name="diff_attn",
    )(lamv, q, k, vt, subln_g.reshape(LANES, 1))


def _sb_block(z, u, cs, mask):
    sp = _softplus2(z)
    if mask is not None:
        sp = jnp.where(mask, sp, 0.0)
    rex = jnp.dot(sp.astype(BF16), u, preferred_element_type=F32)
    a = jnp.exp2((z - sp) - rex - cs)
    if mask is not None:
        a = jnp.where(mask, a, 0.0)
    return a, jnp.sum(sp, axis=1, keepdims=True)


def _sb_kernel(q_ref, k_ref, v_ref, u_ref, o_ref, cs_ref, acc_ref, *, tile):
    qi = pl.program_id(1)
    pairs = q_ref.shape[1]
    refs = (q_ref, k_ref, v_ref, u_ref, cs_ref, acc_ref)
    slowest = functools.reduce(jnp.minimum, [_sb_near(hp, qi, *refs, tile=tile) for hp in range(pairs)])

    @pl.when(jnp.logical_and(qi > SB_NEAR_BLOCKS, jnp.min(slowest) < SB_DONE))
    def _():
        def one_pair(hp, carry):
            _sb_far(hp, qi, *refs, tile=tile)
            return carry

        lax.fori_loop(0, pairs, one_pair, 0)

    for hp in range(pairs):
        o_ref[0, hp] = acc_ref[hp].astype(o_ref.dtype)


def _sb_pair_ops(hp, q_ref, k_ref, v_ref, u_ref, tile):
    qe = _lane_halves(q_ref[0, hp])

    def load(kb):
        start = pl.multiple_of(kb * tile, tile)
        return k_ref[0, hp, pl.ds(start, tile), :], v_ref[0, hp, pl.ds(start, tile), :]

    def block(e, k, v, cs, mask):
        z = lax.dot_general(qe[e], k, _NT, preferred_element_type=F32)
        a, sp_sum = _sb_block(z, u_ref[...], jnp.tile(cs, (1, tile // LANES)), mask)
        return jnp.dot(a.astype(BF16), v, preferred_element_type=F32), sp_sum

    return load, block


def _sb_near(hp, qi, q_ref, k_ref, v_ref, u_ref, cs_ref, acc_ref, *, tile):
    load, block = _sb_pair_ops(hp, q_ref, k_ref, v_ref, u_ref, tile)
    lane = lax.broadcasted_iota(jnp.int32, (tile, LANES), 1)
    row = lax.broadcasted_iota(jnp.int32, (tile, tile), 0)
    col = lax.broadcasted_iota(jnp.int32, (tile, tile), 1)
    k_d, v_d = load(qi)
    earlier = []
    for j in range(1, SB_NEAR_BLOCKS + 1):
        k_p, v_p = load(jnp.maximum(qi - j, 0))
        exists = qi >= j
        earlier.append((k_p, jnp.where(exists, v_p, jnp.zeros_like(v_p)), exists))
    pv, sums = [], []
    for e in range(2):
        pv_e, cs = block(e, k_d, v_d, jnp.zeros((tile, LANES), F32), col < row)
        cs = jnp.broadcast_to(cs, (tile, LANES))
        for k_p, v_p, exists in earlier:
            pv_p, sum_p = block(e, k_p, v_p, cs, None)
            pv_e = pv_e + pv_p
            cs = cs + jnp.where(exists, sum_p, 0.0)
        pv.append(pv_e)
        sums.append(cs)
        cs_ref[hp, e] = cs
    acc_ref[hp] = jnp.where(lane < HALF, pv[0], pv[1])
    return jnp.minimum(sums[0], sums[1])


def _sb_far(hp, qi, q_ref, k_ref, v_ref, u_ref, cs_ref, acc_ref, *, tile):
    load, block = _sb_pair_ops(hp, q_ref, k_ref, v_ref, u_ref, tile)
    lane = lax.broadcasted_iota(jnp.int32, (tile, LANES), 1)

    def least_decay():
        return jnp.min(jnp.minimum(cs_ref[hp, 0], cs_ref[hp, 1]))

    def more(carry):
        kb, decay = carry
        return jnp.logical_and(kb >= 0, decay < SB_DONE)

    def full_block(carry):
        kb, _ = carry
        k, v = load(kb)
        pv, sums = [], []
        for e in range(2):
            cs = cs_ref[hp, e]
            pv_e, sp_sum = block(e, k, v, cs, None)
            pv.append(pv_e)
            sums.append(cs + sp_sum)
            cs_ref[hp, e] = sums[e]
        acc_ref[hp] += jnp.where(lane < HALF, pv[0], pv[1])
        return kb - 1, jnp.min(jnp.minimum(sums[0], sums[1]))

    lax.while_loop(more, full_block, (qi - 1 - SB_NEAR_BLOCKS, least_decay()))


def _sb_attn(q, k, v):
    b, pairs, s, _ = q.shape
    t = SB_TILE
    kv_spec = pl.BlockSpec((1, pairs, s, LANES), lambda bi, qi: (bi, 0, 0, 0))
    q_spec = pl.BlockSpec((1, pairs, t, LANES), lambda bi, qi: (bi, 0, qi, 0))
    return pl.pallas_call(
        functools.partial(_sb_kernel, tile=t),
        out_shape=jax.ShapeDtypeStruct(q.shape, BF16),
        grid=(b, s // t),
        in_specs=[q_spec, kv_spec, kv_spec, pl.BlockSpec((t, t), lambda bi, qi: (0, 0))],
        out_specs=q_spec,
        scratch_shapes=[pltpu.VMEM((pairs, 2, t, LANES), F32), pltpu.VMEM((pairs, t, LANES), F32)],
        compiler_params=_params(2),
        name="sb_attn",
    )(q, k, v, _suffix_matrix(t))


def _diff_sample_kernel(lam_ref, q_ref, kn_ref, vn_ref, ck_ref, cv_ref, g_ref, o_ref, *, past, lam_init):
    ns = q_ref.shape[1]
    lam = _diff_lambda(lam_ref, lam_init)
    qpos = past + lax.broadcasted_iota(jnp.int32, (2 * ns, 1), 0) % ns
    cache_ok = (lax.broadcasted_iota(jnp.int32, (2 * ns, past), 1) // CHUNK) <= (qpos // CHUNK)
    new_ok = ((past + lax.broadcasted_iota(jnp.int32, (2 * ns, ns), 1)) // CHUNK) <= (qpos // CHUNK)
    for h in range(N_HEADS_A):
        hs = slice(h * LANES, (h + 1) * LANES)
        qq = jnp.concatenate(_lane_halves(q_ref[0, :, hs]), axis=0)
        kc = ck_ref[0, :, hs].astype(BF16)
        vc = cv_ref[0, :, h, :].astype(BF16)
        s_c = jnp.where(cache_ok, lax.dot_general(qq, kc, _NT, preferred_element_type=F32), NEG)
        s_n = jnp.where(new_ok, lax.dot_general(qq, kn_ref[0, :, hs], _NT, preferred_element_type=F32), NEG)
        m = jnp.maximum(jnp.max(s_c, axis=1, keepdims=True), jnp.max(s_n, axis=1, keepdims=True))
        p_c = jnp.exp2(s_c - m)
        p_n = jnp.exp2(s_n - m)
        l = jnp.sum(p_c, axis=1, keepdims=True) + jnp.sum(p_n, axis=1, keepdims=True)
        acc = (jnp.dot(p_c.astype(BF16), vc, preferred_element_type=F32)
               + jnp.dot(p_n.astype(BF16), vn_ref[0, :, hs], preferred_element_type=F32))
        o = acc / l
        o_ref[0, :, hs] = _diff_finish(o[:ns], o[ns:], lam, g_ref[...], lam_init).astype(o_ref.dtype)


def _diff_sample(lamv, q, kn, vn, ck, cv, subln_g, lam_init):
    b, ns, _ = q.shape
    past = ck.shape[1]
    new = pl.BlockSpec((1, ns, WIDTH_A), lambda i: (i, 0, 0))
    cache = pl.BlockSpec((1, past, WIDTH_A), lambda i: (i, 0, 0))
    return pl.pallas_call(
        functools.partial(_diff_sample_kernel, past=past, lam_init=lam_init),
        out_shape=jax.ShapeDtypeStruct((b, ns, WIDTH_A), BF16),
        grid=(b,),
        in_specs=[pl.BlockSpec(lamv.shape, lambda i: (0, 0)), new, new, new, cache,
                  pl.BlockSpec((1, past, N_HEADS_A, LANES), lambda i: (i, 0, 0, 0)),
                  pl.BlockSpec((1, LANES), lambda i: (0, 0))],
        out_specs=new,
        compiler_params=_params(1),
        name="diff_sample",
    )(lamv, q, kn, vn, ck, cv, subln_g.reshape(1, LANES))


def _sb_sample_kernel(q_ref, kn_ref, vn_ref, ck_ref, cv_ref, un_ref, uc_ref, o_ref, *, past):
    ns = q_ref.shape[1]
    blk = CACHE_BLOCK
    t = lax.broadcasted_iota(jnp.int32, (2 * ns, ns), 0) % ns
    new_ok = lax.broadcasted_iota(jnp.int32, (2 * ns, ns), 1) < t
    lane = lax.broadcasted_iota(jnp.int32, (ns, LANES), 1)
    for hp in range(WIDTH_B // LANES):
        hs = slice(hp * LANES, (hp + 1) * LANES)
        qq = jnp.concatenate(_lane_halves(q_ref[0, :, hs]), axis=0)
        z_n = lax.dot_general(qq, kn_ref[0, :, hs], _NT, preferred_element_type=F32)
        a, cs = _sb_block(z_n, un_ref[...], 0.0, new_ok)
        acc = jnp.dot(a.astype(BF16), vn_ref[0, :, hs], preferred_element_type=F32)
        for kb in reversed(range(past // blk)):
            ks = slice(kb * blk, (kb + 1) * blk)
            z = lax.dot_general(qq, ck_ref[0, ks, hs].astype(BF16), _NT, preferred_element_type=F32)
            a, sp_sum = _sb_block(z, uc_ref[...], cs, None)
            acc += jnp.dot(a.astype(BF16), cv_ref[0, ks, hs].astype(BF16), preferred_element_type=F32)
            cs = cs + sp_sum
        o_ref[0, :, hs] = jnp.where(lane < HALF, acc[:ns], acc[ns:]).astype(o_ref.dtype)


def _sb_sample(q, kn, vn, ck, cv):
    b, ns, _ = q.shape
    past = ck.shape[1]
    new = pl.BlockSpec((1, ns, WIDTH_B), lambda i: (i, 0, 0))
    cache = pl.BlockSpec((1, past, WIDTH_B), lambda i: (i, 0, 0))
    const = lambda n: pl.BlockSpec((n, n), lambda i: (0, 0))
    return pl.pallas_call(
        functools.partial(_sb_sample_kernel, past=past),
        out_shape=jax.ShapeDtypeStruct((b, ns, WIDTH_B), BF16),
        grid=(b,),
        in_specs=[new, new, new, cache, cache, const(ns), const(CACHE_BLOCK)],
        out_specs=new,
        compiler_params=_params(1),
        name="sb_sample",
    )(q, kn, vn, ck, cv, _suffix_matrix(ns), _suffix_matrix(CACHE_BLOCK))


def _merge_kernel(x_ref, mod_ref, oa_ref, ob_ref, sza_ref, szb_ref, sga_ref, sgb_ref,
                  wa_ref, wb_ref, wo_ref, fg_ref, y_ref, *, final_norm):
    bb, ts, d = x_ref.shape
    n = bb * ts

    def flat(ref):
        if len(ref.shape) == 4:
            return jnp.concatenate([ref[0, i] for i in range(ref.shape[1])], axis=1)
        return ref[...].reshape(n, ref.shape[-1])

    ya =jnp.dot(flat(oa_ref) * flat(sza_ref), wa_ref[...], preferred_element_type=F32)
    yb = jnp.dot(flat(ob_ref) * flat(szb_ref), wb_ref[...], preferred_element_type=F32)
    mixed = flat(sga_ref).astype(F32) * ya + flat(sgb_ref).astype(F32) * yb
    out = jnp.dot(mixed.astype(BF16), wo_ref[...], preferred_element_type=F32)
    xn = x_ref[...] + mod_ref[...][:, :, 2 * d:] * out.reshape(bb, ts, d)
    if final_norm:
        xn = xn * lax.rsqrt(jnp.mean(xn * xn, axis=-1, keepdims=True) + NORM_EPS) * fg_ref[...]
    y_ref[...] = xn


def _merge(x, mod, oa, ob, sza, szb, sga, sgb, wa16, wb16, wo16, final_g, bb, ts, final_norm):
    b, s, d = x.shape
    tok = lambda width: pl.BlockSpec((bb, ts, width), lambda i, j: (i, j, 0))
    const = lambda arr: pl.BlockSpec(arr.shape, lambda i, j: (0,) * arr.ndim)

    def attn_out(arr):
        if arr.ndim == 4:
            return pl.BlockSpec((1, arr.shape[1], ts, LANES), lambda i, j: (i, 0, j, 0))
        return tok(arr.shape[-1])

    fg = final_g.reshape(1, 1, d)
    return pl.pallas_call(
        functools.partial(_merge_kernel, final_norm=final_norm),
        out_shape=jax.ShapeDtypeStruct((b, s, d), F32),
        grid=(b // bb, s // ts),
        in_specs=[tok(d), pl.BlockSpec((bb, 1, 3 * d), lambda i, j: (i, 0, 0)),
                  attn_out(oa), attn_out(ob), tok(WIDTH_A), tok(WIDTH_B), tok(d), tok(d),
                  const(wa16), const(wb16), const(wo16), const(fg)],
        out_specs=tok(d),
        compiler_params=_params(2),
        name="merge",
    )(x, mod, oa, ob, sza, szb, sga, sgb, wa16, wb16, wo16, fg)


def _rope_tables(pos):
    half = HEAD_DIM_A // 2
    inv = ROPE_THETA ** (-jnp.arange(half, dtype=F32) / half)
    ang = pos.astype(F32)[:, None] * inv[None, :]
    cos, sin = jnp.cos(ang), jnp.sin(ang)
    reps = LANES // HEAD_DIM_A
    return (jnp.tile(jnp.concatenate([cos, cos], axis=1), (1, reps)),
            jnp.tile(jnp.concatenate([-sin, sin], axis=1), (1, reps)))


def kernel(x_prompt, x_sample, c_prompt, c_sample, cache_a_k, cache_a_v, cache_b_k, cache_b_v, norm_g, w_ada, b_ada, w_in, lambda_q1, lambda_k1, lambda_q2, lambda_k2, subln_g, w_branch_a, w_branch_b, w_out, final_g):
    depth = w_in.shape[0]
    bp, sp, d = x_prompt.shape
    bs, ns, _ = x_sample.shape
    past = cache_a_k.shape[2]
    assert sp % PROJ_TOKENS == 0 and PROJ_TOKENS == DIFF_TILE and sp % SB_TILE == 0 and past % CACHE_BLOCK == 0
    assert ns % 16 == 0 and subln_g.shape[1] == LANES

    cos_p, sin_p = _rope_tables(jnp.arange(sp, dtype=jnp.int32))
    cos_s, sin_s = _rope_tables(past + jnp.arange(ns, dtype=jnp.int32))
    c_all = jnp.concatenate([c_prompt, c_sample], axis=0)

    xp, xs = x_prompt, x_sample
    caches = [[], [], [], [], [], [], [], []]
    for l in range(depth):
        lam_init = 0.8 - 0.6 * math.exp(-0.3 * l)
        final = l == depth - 1
        mod = _ada(c_all, w_ada[l], b_ada[l])
        mod_p = mod[:bp].reshape(bp, 1, 3 * d)
        mod_s = mod[bp:].reshape(bs, 1, 3 * d)
        w16 = w_in[l].astype(BF16)
        wa16, wb16, wo16 = (w[l].astype(BF16) for w in (w_branch_a, w_branch_b, w_out))
        lamv = jnp.stack([lambda_q1[l], lambda_k1[l], lambda_q2[l], lambda_k2[l]])

        (ka32, va32, kb32, vb32, qa, ka, va, qb, kb, vb, sza, szb, sga, sgb) = _proj(
            xp, mod_p, norm_g[l], cos_p, sin_p, w16, 1, PROJ_TOKENS, True)
        oa = _diff_attn(lamv, qa, ka, va, subln_g[l], lam_init)
        ob = _sb_attn(qb, kb, vb)
        xp = _merge(xp, mod_p, oa, ob, sza, szb, sga, sgb, wa16, wb16, wo16, final_g, 1, PROJ_TOKENS, final)
        for dst, val in zip(caches[:4], (ka32, va32, kb32, vb32)):
            dst.append(val)

        (ka32, va32, kb32, vb32, qa, ka, va, qb, kb, vb, sza, szb, sga, sgb) = _proj(
            xs, mod_s, norm_g[l], cos_s, sin_s, w16, bs, ns, False)
        oa = _diff_sample(lamv, qa, ka, va, cache_a_k[l].reshape(bs, past, WIDTH_A),
                          cache_a_v[l], subln_g[l], lam_init)
        ob = _sb_sample(qb, kb, vb, cache_b_k[l].reshape(bs, past, WIDTH_B),
                        cache_b_v[l].reshape(bs, past, WIDTH_B))
        xs = _merge(xs, mod_s, oa, ob, sza, szb, sga, sgb, wa16, wb16, wo16, final_g, bs, ns, final)
        for dst, val in zip(caches[4:], (ka32, va32, kb32, vb32)):
            dst.append(val)

    def stacked(vals, shape_tail):
        arr = vals[0][None] if depth == 1 else jnp.stack(vals)
        return arr.reshape(arr.shape[:3] + shape_tail)

    tails = [(N_HEADS_A, 2, HEAD_DIM_A), (N_HEADS_A, 2 * HEAD_DIM_A),
             (N_HEADS_B, HEAD_DIM_B), (N_HEADS_B, HEAD_DIM_B)] * 2
    return (xp, xs) + tuple(stacked(v, t) for v, t in zip(caches, tails))
```

```python
import functools
import math

import jax
import jax.numpy as jnp
from jax import lax
from jax.experimental import pallas as pl
from jax.experimental.pallas import tpu as pltpu

F32 = jnp.float32
BF16 = jnp.bfloat16

CHUNK = 64
N_HEADS_A = 4
HEAD_DIM_A = 64
WIDTH_A = N_HEADS_A * 2 * HEAD_DIM_A
N_HEADS_B = 8
HEAD_DIM_B = 64
WIDTH_B = N_HEADS_B * HEAD_DIM_B
ROPE_THETA = 10000.0
NORM_EPS = 1e-6

LANES = 128
HALF = 64
VMEM_LIMIT = 56 * 1024 * 1024
NEG = -1e30
SB_DONE = 150.0

PROJ_TOKENS = 512
DIFF_TILE = 512
SB_TILE = 256
SB_NEAR_BLOCKS = 1
LOG2E = math.log2(math.e)
CACHE_BLOCK = 256

_NT = (((1,), (1,)), ((), ()))


def _params(n_axes):
    return pltpu.CompilerParams(dimension_semantics=("parallel",) * n_axes,
                                vmem_limit_bytes=VMEM_LIMIT)


def _lane_halves(x):
    lane = lax.broadcasted_iota(jnp.int32, x.shape, x.ndim - 1)
    zero = jnp.zeros_like(x)
    return jnp.where(lane < HALF, x, zero), jnp.where(lane >= HALF, x, zero)


def _softplus2(z):
    return jnp.maximum(z, 0.0) + jnp.log2(1.0 + jnp.exp2(jnp.minimum(z, -z)))


def _suffix_matrix(n):
    j = lax.broadcasted_iota(jnp.int32, (n, n), 0)
    s = lax.broadcasted_iota(jnp.int32, (n, n), 1)
    return (j > s).astype(BF16)


def _ada_kernel(c_ref, w_ref, b_ref, o_ref):
    o_ref[...] = jnp.dot(c_ref[...], w_ref[...], preferred_element_type=F32,
                         precision=lax.Precision.HIGHEST) + b_ref[...]


def _ada(c, w, b):
    n, d = c.shape
    width = w.shape[1]
    tn = 512
    return pl.pallas_call(
        _ada_kernel,
        out_shape=jax.ShapeDtypeStruct((n, width), F32),
        grid=(width // tn,),
        in_specs=[pl.BlockSpec((n, d), lambda j: (0, 0)),
                  pl.BlockSpec((d, tn), lambda j: (0, j)),
                  pl.BlockSpec((1, tn), lambda j: (0, j))],
        out_specs=pl.BlockSpec((n, tn), lambda j: (0, j)),
        compiler_params=_params(1),
        name="ada",
    )(c, w, b.reshape(1, width))


def _rope(x, cos, sin_signed):
    lane = lax.broadcasted_iota(jnp.int32, x.shape, 1)
    first_half = (lane % HALF) < (HALF // 2)
    partner = jnp.where(first_half, pltpu.roll(x, LANES - HALF // 2, 1), pltpu.roll(x, HALF // 2, 1))
    return x * cos + partner * sin_signed


def _proj_kernel(x_ref, mod_ref, g_ref, cos_ref, sin_ref, w_ref,
                 ka32, va32, kb32, vb32, qa16, ka16, va16, qb16, kb16, vb16, sza, szb, sga, sgb):
    bb, ts, d = x_ref.shape
    n = bb * ts
    x = x_ref[...]
    y = x * lax.rsqrt(jnp.mean(x * x, axis=-1, keepdims=True) + NORM_EPS) * g_ref[...]
    mod = mod_ref[...]
    h = (y * (1.0 + mod[:, :, d:2 * d]) + mod[:, :, :d]).reshape(n, d).astype(BF16)
    cos = jnp.broadcast_to(cos_ref[...][None], (bb, ts, LANES)).reshape(n, LANES)
    sin = jnp.broadcast_to(sin_ref[...][None], (bb, ts, LANES)).reshape(n, LANES)

    def cols(start, width):
        return jnp.dot(h, w_ref[:, start:start + width], preferred_element_type=F32)

    def put(ref, val):
        if len(ref.shape) == 3:
            ref[...] = val.reshape(ref.shape).astype(ref.dtype)
            return
        src = val.T if len(ref.shape) == 5 else val
        for i in range(ref.shape[1]):
            if len(ref.shape) == 5:
                ref[0, i, 0] = src[i * LANES:(i + 1) * LANES].astype(ref.dtype)
            else:
                ref[0, i] = src[:, i * LANES:(i + 1) * LANES].astype(ref.dtype)

    def rope_all(val):
        return jnp.concatenate(
            [_rope(val[:, i * LANES:(i + 1) * LANES], cos, sin) for i in range(WIDTH_A // LANES)], axis=1)

    wa, wb = WIDTH_A, WIDTH_B
    put(qa16, rope_all(cols(0, wa)) * (HEAD_DIM_A ** -0.5 * LOG2E))
    ka = rope_all(cols(wa, wa))
    put(ka32, ka)
    put(ka16, ka)
    va = cols(2 * wa, wa)
    if len(va32.shape) == 4:
        for i in range(N_HEADS_A):
            va32[0, :, i, :] = va[:, i * LANES:(i + 1) * LANES]
    else:
        put(va32, va)
    put(va16, va)
    za = cols(3 * wa, wa)
    put(sza, za * jax.nn.sigmoid(za))
    off = 4 * wa
    put(qb16, cols(off, wb) * (HEAD_DIM_B ** -0.5 * LOG2E))
    kb = cols(off + wb, wb)
    put(kb32, kb)
    put(kb16, kb)
    vb = cols(off + 2 * wb, wb)
    put(vb32, vb)
    put(vb16, vb)
    zb = cols(off + 3 * wb, wb)
    put(szb, zb * jax.nn.sigmoid(zb))
    off += 4 * wb
    put(sga, jax.nn.sigmoid(cols(off, d)))
    put(sgb, jax.nn.sigmoid(cols(off + d, d)))


def _proj(x, mod, norm_g, cos, sin, w16, bb, ts, head_major):
    b, s, d = x.shape
    grid = (b // bb, s // ts)
    tok = lambda width: pl.BlockSpec((bb, ts, width), lambda i, j: (i, j, 0))
    shape = lambda width, dt: jax.ShapeDtypeStruct((b, s, width), dt)
    widths32 = [WIDTH_A, WIDTH_A, WIDTH_B, WIDTH_B]
    widths16 = [WIDTH_A] * 3 + [WIDTH_B] * 3 + [WIDTH_A, WIDTH_B, d, d]
    out_shape = [shape(w, F32) for w in widths32] + [shape(w, BF16) for w in widths16]
    out_specs = [tok(w) for w in widths32 + widths16]
    if head_major:
        assert bb == 1
        for o in range(4, 10):
            groups = widths16[o - 4] // LANES
            out_shape[o] = jax.ShapeDtypeStruct((b, groups, s, LANES), BF16)
            out_specs[o] = pl.BlockSpec((1, groups, ts, LANES), lambda i, j: (i, 0, j, 0))
        out_shape[1] = jax.ShapeDtypeStruct((b, s, N_HEADS_A, LANES), F32)
        out_specs[1] = pl.BlockSpec((1, ts, N_HEADS_A, LANES), lambda i, j: (i, j, 0, 0))
        out_shape[6] = jax.ShapeDtypeStruct((b, N_HEADS_A, s // ts, LANES, ts), BF16)
        out_specs[6] = pl.BlockSpec((1, N_HEADS_A, 1, LANES, ts), lambda i, j: (i, 0, j, 0, 0))
    return pl.pallas_call(
        _proj_kernel,
        out_shape=out_shape,
        grid=grid,
        in_specs=[tok(d),
                  pl.BlockSpec((bb, 1, 3 * d), lambda i, j: (i, 0, 0)),
                  pl.BlockSpec((1, 1, d), lambda i, j: (0, 0, 0)),
                  pl.BlockSpec((ts, LANES), lambda i, j: (j, 0)),
                  pl.BlockSpec((ts, LANES), lambda i, j: (j, 0)),
                  pl.BlockSpec(w16.shape, lambda i, j: (0, 0), pipeline_mode=pl.Buffered(1))],
        out_specs=out_specs,
        compiler_params=_params(2),
        name="proj",
    )(x, mod, norm_g.reshape(1, 1, d), cos, sin, w16)


def _diff_lambda(lam_ref, lam_init):
    lv = lam_ref[...]
    a = jnp.sum(lv[0:1] * lv[1:2], axis=1, keepdims=True)
    b = jnp.sum(lv[2:3] * lv[3:4], axis=1, keepdims=True)
    return jnp.exp(a) - jnp.exp(b) + lam_init


def _diff_finish(o0, o1, lam, g, lam_init):
    o = o0 - lam * o1
    on = o * lax.rsqrt(jnp.mean(o * o, axis=1, keepdims=True) + NORM_EPS) * g
    return on * (1.0 - lam_init)


def _diff_kernel(*refs, tile, lam_init):
    qi = pl.program_id(1)
    for h in range(N_HEADS_A):
        _diff_head(h, qi, *refs, tile=tile, lam_init=lam_init)


def _diff_head(h, qi, lam_ref, q_ref, k_ref, vt_ref, g_ref, o_ref,
               sa_ref, sb_ref, ta_ref, tb_ref, m_ref, l_ref, acc_ref, *, tile, lam_init):
    qc = _lane_halves(q_ref[0, h])
    buf_a, buf_b = (sa_ref, ta_ref), (sb_ref, tb_ref)
    m_ref[...] = jnp.full(m_ref.shape, NEG, F32)
    l_ref[...] = jnp.zeros(l_ref.shape, F32)
    acc_ref[...] = jnp.zeros(acc_ref.shape, F32)

    def produce(kb, buf):
        s_ref, top_ref = buf
        k = k_ref[0, h, pl.ds(pl.multiple_of(kb * tile, tile), tile), :]
        for c in range(2):
            s = lax.dot_general(k, qc[c], _NT, preferred_element_type=F32)
            s_ref[c] = s
            top_ref[c] = jnp.max(s, axis=0, keepdims=True)

    def consume(kb, buf):
        s_ref, top_ref = buf
        vt = vt_ref[0, h, kb]
        alpha, p = [], []
        for c in range(2):
            m_prev = m_ref[c]
            m_next = jnp.maximum(m_prev, top_ref[c])
            alpha.append(jnp.exp2(m_prev - m_next))
            p.append(jnp.exp2(s_ref[c] - m_next))
            l_ref[c] = alpha[c] * l_ref[c] + jnp.sum(p[c], axis=0, keepdims=True)
            m_ref[c] = m_next
        for c in range(2):
            acc_ref[c] = alpha[c] * acc_ref[c] + jnp.dot(vt, p[c].astype(BF16), preferred_element_type=F32)

    def two_blocks(i, carry):
        kb = 2 * i
        produce(kb + 1, buf_b)
        consume(kb, buf_a)
        produce(kb + 2, buf_a)
        consume(kb + 1, buf_b)
        return carry

    half = tile // 2

    def produce_diagonal(buf):
        s_ref, _ = buf
        k = k_ref[0, h, pl.ds(pl.multiple_of(qi * tile, tile), tile), :]
        for c in range(2):
            s_ref[c, :half, :] = lax.dot_general(k[:half], qc[c], _NT, preferred_element_type=F32)
            s_ref[c, half:, half:] = lax.dot_general(k[half:], qc[c][half:], _NT, preferred_element_type=F32)

    def consume_diagonal(buf):
        s_ref, _ = buf
        vt = vt_ref[0, h, qi]
        key = lax.broadcasted_iota(jnp.int32, (half, tile), 0)
        query = lax.broadcasted_iota(jnp.int32, (half, tile), 1)
        visible = (key // CHUNK) <= (query // CHUNK)
        for c in range(2):
            s0 = jnp.where(visible, s_ref[c, :half, :], NEG)
            s1 = jnp.where(visible[:, :half], s_ref[c, half:, half:], NEG)
            top0 = jnp.max(s0, axis=0, keepdims=True)
            top = jnp.concatenate(
                [top0[:, :half], jnp.maximum(top0[:, half:], jnp.max(s1, axis=0, keepdims=True))], axis=1)
            m_prev = m_ref[c]
            m_next = jnp.maximum(m_prev, top)
            alpha = jnp.exp2(m_prev - m_next)
            p0 = jnp.exp2(s0 - m_next)
            p1 = jnp.exp2(s1 - m_next[:, half:])
            sum0 = jnp.sum(p0, axis=0, keepdims=True)
            l_prev = alpha * l_ref[c]
            l_ref[c, :, :half] = l_prev[:, :half] + sum0[:, :half]
            l_ref[c, :, half:] = l_prev[:, half:] + sum0[:, half:] + jnp.sum(p1, axis=0, keepdims=True)
            pv0 = jnp.dot(vt[:, :half], p0.astype(BF16), preferred_element_type=F32)
            pv1 = jnp.dot(vt[:, half:], p1.astype(BF16), preferred_element_type=F32)
            acc_prev = alpha * acc_ref[c]
            acc_ref[c, :, :half] = acc_prev[:, :half] + pv0[:, :half]
            acc_ref[c, :, half:] = acc_prev[:, half:] + pv0[:, half:] + pv1
            m_ref[c] = m_next

    produce(0, buf_a)
    lax.fori_loop(0, qi // 2, two_blocks, 0)

    @pl.when(qi % 2 == 1)
    def _():
        produce_diagonal(buf_b)
        consume(qi - 1, buf_a)
        consume_diagonal(buf_b)

    @pl.when(qi % 2 == 0)
    def _():
        consume_diagonal(buf_a)

    lam = _diff_lambda(lam_ref, lam_init)
    o = acc_ref[0] * (1.0 / l_ref[0]) - acc_ref[1] * (lam / l_ref[1])
    on = o * lax.rsqrt(jnp.mean(o * o, axis=0, keepdims=True) + NORM_EPS) * g_ref[...]
    o_ref[0, h] = (on * (1.0 - lam_init)).T.astype(o_ref.dtype)


def _diff_attn(lamv, q, k, vt, subln_g, lam_init):
    b, heads, s, _ = q.shape
    t = DIFF_TILE
    q_spec = pl.BlockSpec((1, heads, t, LANES), lambda bi, qi: (bi, 0, qi, 0))
    return pl.pallas_call(
        functools.partial(_diff_kernel, tile=t, lam_init=lam_init),
        out_shape=jax.ShapeDtypeStruct(q.shape, BF16),
        grid=(b, s // t),
        in_specs=[pl.BlockSpec(lamv.shape, lambda bi, qi: (0, 0)),
                  q_spec,
                  pl.BlockSpec((1, heads, s, LANES), lambda bi, qi: (bi, 0, 0, 0)),
                  pl.BlockSpec((1, heads, s // t, LANES, t), lambda bi, qi: (bi, 0, 0, 0, 0)),
                  pl.BlockSpec((LANES, 1), lambda bi, qi: (0, 0))],
        out_specs=q_spec,
        scratch_shapes=[pltpu.VMEM((2, t, t), F32)] * 2
        + [pltpu.VMEM((2, 1, t), F32)] * 4 + [pltpu.VMEM((2, LANES, t), F32)],
        compiler_params=_params(2),
        name="diff_attn",
    )(lamv, q, k, vt, subln_g.reshape(LANES, 1))


def _sb_block(z, u, cs, mask):
    sp = _softplus2(z)
    if mask is not None:
        sp = jnp.where(mask, sp, 0.0)
    rex = jnp.dot(sp.astype(BF16), u, preferred_element_type=F32)
    a = jnp.exp2((z - sp) - rex - cs)
    if mask is not None:
        a = jnp.where(mask, a, 0.0)
    return a, jnp.sum(sp, axis=1, keepdims=True)


def _sb_kernel(q_ref, k_ref, v_ref, u_ref, o_ref, cs_ref, acc_ref, *, tile):
    qi = pl.program_id(1)
    pairs = q_ref.shape[1]
    refs = (q_ref, k_ref, v_ref, u_ref, cs_ref, acc_ref)
    slowest = _sb_near_all(qi, *refs, tile=tile)

    @pl.when(jnp.logical_and(qi > SB_NEAR_BLOCKS, jnp.min(slowest) < SB_DONE))
    def _():
        def one_pair(hp, carry):
            _sb_far(hp, qi, *refs, tile=tile)
            return carry

        lax.fori_loop(0, pairs, one_pair, 0)

    for hp in range(pairs):
        o_ref[0, hp] = acc_ref[hp].astype(o_ref.dtype)


def _sb_near_all(qi, q_ref, k_ref, v_ref, u_ref, cs_ref, acc_ref, *, tile):
    pairs = q_ref.shape[1]
    lane = lax.broadcasted_iota(jnp.int32, (tile, LANES), 1)
    row = lax.broadcasted_iota(jnp.int32, (tile, tile), 0)
    col = lax.broadcasted_iota(jnp.int32, (tile, tile), 1)
    causal = col < row
    u = u_ref[...]
    blocks = [(qi, None)] + [(jnp.maximum(qi - j, 0), qi >= j) for j in range(1, SB_NEAR_BLOCKS + 1)]
    chains = [(hp, e, j) for hp in range(pairs) for e in range(2) for j in range(len(blocks))]

    z, values = {}, {}
    for hp in range(pairs):
        qe = _lane_halves(q_ref[0, hp])
        for j, (kb, exists) in enumerate(blocks):
            rows = pl.ds(pl.multiple_of(kb * tile, tile), tile)
            v = v_ref[0, hp, rows, :]
            values[hp, j] = v if exists is None else jnp.where(exists, v, jnp.zeros_like(v))
            for e in range(2):
                z[hp, e, j] = lax.dot_general(qe[e], k_ref[0, hp, rows, :], _NT, preferred_element_type=F32)

    sp, sums = {}, {}
    for c in chains:
        s = _softplus2(z[c])
        sp[c] = jnp.where(causal, s, 0.0) if c[2] == 0 else s
        total = jnp.broadcast_to(jnp.sum(sp[c], axis=1, keepdims=True), (tile, LANES))
        exists = blocks[c[2]][1]
        sums[c] = total if exists is None else jnp.where(exists, total, 0.0)

    rex = {c: jnp.dot(sp[c].astype(BF16), u, preferred_element_type=F32) for c in chains}

    weights, cs = {}, {}
    for hp in range(pairs):
        for e in range(2):
            after = jnp.zeros((tile, LANES), F32)
            for j in range(len(blocks)):
                c = (hp, e, j)
                a = jnp.exp2((z[c] - sp[c]) - rex[c] - jnp.tile(after, (1, tile // LANES)))
                weights[c] = jnp.where(causal, a, 0.0) if j == 0 else a
                after = after + sums[c]
            cs[hp, e] = after
            cs_ref[hp, e] = after

    slowest = None
    for hp in range(pairs):
        pv = [sum(jnp.dot(weights[hp, e, j].astype(BF16), values[hp, j], preferred_element_type=F32)
                  for j in range(len(blocks))) for e in range(2)]
        acc_ref[hp] = jnp.where(lane < HALF, pv[0], pv[1])
        pair_min = jnp.minimum(cs[hp, 0], cs[hp, 1])
        slowest = pair_min if slowest is None else jnp.minimum(slowest, pair_min)
    return slowest


def _sb_pair_ops(hp, q_ref, k_ref, v_ref, u_ref, tile):
    qe = _lane_halves(q_ref[0, hp])

    def load(kb):
        start = pl.multiple_of(kb * tile, tile)
        return k_ref[0, hp, pl.ds(start, tile), :], v_ref[0, hp, pl.ds(start, tile), :]

    def block(e, k, v, cs, mask):
        z = lax.dot_general(qe[e], k, _NT, preferred_element_type=F32)
        a, sp_sum = _sb_block(z, u_ref[...], jnp.tile(cs, (1, tile // LANES)), mask)
        return jnp.dot(a.astype(BF16), v, preferred_element_type=F32), sp_sum

    return load, block


def _sb_far(hp, qi, q_ref, k_ref, v_ref, u_ref, cs_ref, acc_ref, *, tile):
    load, block = _sb_pair_ops(hp, q_ref, k_ref, v_ref, u_ref, tile)
    lane = lax.broadcasted_iota(jnp.int32, (tile, LANES), 1)

    def least_decay():
        return jnp.min(jnp.minimum(cs_ref[hp, 0], cs_ref[hp, 1]))

    def more(carry):
        kb, decay = carry
        return jnp.logical_and(kb >= 0, decay < SB_DONE)

    def full_block(carry):
        kb, _ = carry
        k, v = load(kb)
        pv, sums = [], []
        for e in range(2):
            cs = cs_ref[hp, e]
            pv_e, sp_sum = block(e, k, v, cs, None)
            pv.append(pv_e)
            sums.append(cs + sp_sum)
            cs_ref[hp, e] = sums[e]
        acc_ref[hp] += jnp.where(lane < HALF, pv[0], pv[1])
        return kb - 1, jnp.min(jnp.minimum(sums[0], sums[1]))

    lax.while_loop(more, full_block, (qi - 1 - SB_NEAR_BLOCKS, least_decay()))


def _sb_attn(q, k, v):
    b, pairs, s, _ = q.shape
    t = SB_TILE
    kv_spec = pl.BlockSpec((1, pairs, s, LANES), lambda bi, qi: (bi, 0, 0, 0))
    q_spec = pl.BlockSpec((1, pairs, t, LANES), lambda bi, qi: (bi, 0, qi, 0))
    return pl.pallas_call(
        functools.partial(_sb_kernel, tile=t),
        out_shape=jax.ShapeDtypeStruct(q.shape, BF16),
        grid=(b, s // t),
        in_specs=[q_spec, kv_spec, kv_spec, pl.BlockSpec((t, t), lambda bi, qi: (0, 0))],
        out_specs=q_spec,
        scratch_shapes=[pltpu.VMEM((pairs, 2, t, LANES), F32), pltpu.VMEM((pairs, t, LANES), F32)],
        compiler_params=_params(2),
        name="sb_attn",
    )(q, k, v, _suffix_matrix(t))


def _diff_sample_kernel(lam_ref, q_ref, kn_ref, vn_ref, ck_ref, cv_ref, g_ref, o_ref, *, past, lam_init):
    ns = q_ref.shape[1]
    lam = _diff_lambda(lam_ref, lam_init)
    qpos = past + lax.broadcasted_iota(jnp.int32, (2 * ns, 1), 0) % ns
    cache_ok = (lax.broadcasted_iota(jnp.int32, (2 * ns, past), 1) // CHUNK) <= (qpos // CHUNK)
    new_ok = ((past + lax.broadcasted_iota(jnp.int32, (2 * ns, ns), 1)) // CHUNK) <= (qpos // CHUNK)
    for h in range(N_HEADS_A):
        hs = slice(h * LANES, (h + 1) * LANES)
        qq = jnp.concatenate(_lane_halves(q_ref[0, :, hs]), axis=0)
        kc = ck_ref[0, :, hs].astype(BF16)
        vc = cv_ref[0, :, h, :].astype(BF16)
        s_c = jnp.where(cache_ok, lax.dot_general(qq, kc, _NT, preferred_element_type=F32), NEG)
        s_n = jnp.where(new_ok, lax.dot_general(qq, kn_ref[0, :, hs], _NT, preferred_element_type=F32), NEG)
        m = jnp.maximum(jnp.max(s_c, axis=1, keepdims=True), jnp.max(s_n, axis=1, keepdims=True))
        p_c = jnp.exp2(s_c - m)
        p_n = jnp.exp2(s_n - m)
        l = jnp.sum(p_c, axis=1, keepdims=True) + jnp.sum(p_n, axis=1, keepdims=True)
        acc = (jnp.dot(p_c.astype(BF16), vc, preferred_element_type=F32)
               + jnp.dot(p_n.astype(BF16), vn_ref[0, :, hs], preferred_element_type=F32))
        o = acc / l
        o_ref[0, :, hs] = _diff_finish(o[:ns], o[ns:], lam, g_ref[...], lam_init).astype(o_ref.dtype)


def _diff_sample(lamv, q, kn, vn, ck, cv, subln_g, lam_init):
    b, ns, _ = q.shape
    past = ck.shape[1]
    new = pl.BlockSpec((1, ns, WIDTH_A), lambda i: (i, 0, 0))
    cache = pl.BlockSpec((1, past, WIDTH_A), lambda i: (i, 0, 0))
    return pl.pallas_call(
        functools.partial(_diff_sample_kernel, past=past, lam_init=lam_init),
        out_shape=jax.ShapeDtypeStruct((b, ns, WIDTH_A), BF16),
        grid=(b,),
        in_specs=[pl.BlockSpec(lamv.shape, lambda i: (0, 0)), new, new, new, cache,
                  pl.BlockSpec((1, past, N_HEADS_A, LANES), lambda i: (i, 0, 0, 0)),
                  pl.BlockSpec((1, LANES), lambda i: (0, 0))],
        out_specs=new,
        compiler_params=_params(1),
        name="diff_sample",
    )(lamv, q, kn, vn, ck, cv, subln_g.reshape(1, LANES))


def _sb_sample_kernel(q_ref, kn_ref, vn_ref, ck_ref, cv_ref, un_ref, uc_ref, o_ref, *, past):
    ns = q_ref.shape[1]
    blk = CACHE_BLOCK
    t = lax.broadcasted_iota(jnp.int32, (2 * ns, ns), 0) % ns
    new_ok = lax.broadcasted_iota(jnp.int32, (2 * ns, ns), 1) < t
    lane = lax.broadcasted_iota(jnp.int32, (ns, LANES), 1)
    for hp in range(WIDTH_B // LANES):
        hs = slice(hp * LANES, (hp + 1) * LANES)
        qq = jnp.concatenate(_lane_halves(q_ref[0, :, hs]), axis=0)
        z_n = lax.dot_general(qq, kn_ref[0, :, hs], _NT, preferred_element_type=F32)
        a, cs = _sb_block(z_n, un_ref[...], 0.0, new_ok)
        acc = jnp.dot(a.astype(BF16), vn_ref[0, :, hs], preferred_element_type=F32)
        for kb in reversed(range(past // blk)):
            ks = slice(kb * blk, (kb + 1) * blk)
            z = lax.dot_general(qq, ck_ref[0, ks, hs].astype(BF16), _NT, preferred_element_type=F32)
            a, sp_sum = _sb_block(z, uc_ref[...], cs, None)
            acc += jnp.dot(a.astype(BF16), cv_ref[0, ks, hs].astype(BF16), preferred_element_type=F32)
            cs = cs + sp_sum
        o_ref[0, :, hs] = jnp.where(lane < HALF, acc[:ns], acc[ns:]).astype(o_ref.dtype)


def _sb_sample(q, kn, vn, ck, cv):
    b, ns, _ = q.shape
    past = ck.shape[1]
    new = pl.BlockSpec((1, ns, WIDTH_B), lambda i: (i, 0, 0))
    cache = pl.BlockSpec((1, past, WIDTH_B), lambda i: (i, 0, 0))
    const = lambda n: pl.BlockSpec((n, n), lambda i: (0, 0))
    return pl.pallas_call(
        functools.partial(_sb_sample_kernel, past=past),
        out_shape=jax.ShapeDtypeStruct((b, ns, WIDTH_B), BF16),
        grid=(b,),
        in_specs=[new, new, new, cache, cache, const(ns), const(CACHE_BLOCK)],
        out_specs=new,
        compiler_params=_params(1),
        name="sb_sample",
    )(q, kn, vn, ck, cv, _suffix_matrix(ns), _suffix_matrix(CACHE_BLOCK))


def _merge_kernel(x_ref, mod_ref, oa_ref, ob_ref, sza_ref, szb_ref, sga_ref, sgb_ref,
                  wa_ref, wb_ref, wo_ref, fg_ref, y_ref, *, final_norm):
    bb, ts, d = x_ref.shape
    n = bb * ts

    def flat(ref):
        if len(ref.shape) == 4:
            return jnp.concatenate([ref[0, i] for i in range(ref.shape[1])], axis=1)
        return ref[...].reshape(n, ref.shape[-1])

    ya =jnp.dot(flat(oa_ref) * flat(sza_ref), wa_ref[...], preferred_element_type=F32)
    yb = jnp.dot(flat(ob_ref) * flat(szb_ref), wb_ref[...], preferred_element_type=F32)
    mixed = flat(sga_ref).astype(F32) * ya + flat(sgb_ref).astype(F32) * yb
    out = jnp.dot(mixed.astype(BF16), wo_ref[...], preferred_element_type=F32)
    xn = x_ref[...] + mod_ref[...][:, :, 2 * d:] * out.reshape(bb, ts, d)
    if final_norm:
        xn = xn * lax.rsqrt(jnp.mean(xn * xn, axis=-1, keepdims=True) + NORM_EPS) * fg_ref[...]
    y_ref[...] = xn


def _merge(x, mod, oa, ob, sza, szb, sga, sgb, wa16, wb16, wo16, final_g, bb, ts, final_norm):
    b, s, d = x.shape
    tok = lambda width: pl.BlockSpec((bb, ts, width), lambda i, j: (i, j, 0))
    const = lambda arr: pl.BlockSpec(arr.shape, lambda i, j: (0,) * arr.ndim)

    def attn_out(arr):
        if arr.ndim == 4:
            return pl.BlockSpec((1, arr.shape[1], ts, LANES), lambda i, j: (i, 0, j, 0))
        return tok(arr.shape[-1])

    fg = final_g.reshape(1, 1, d)
    return pl.pallas_call(
        functools.partial(_merge_kernel, final_norm=final_norm),
        out_shape=jax.ShapeDtypeStruct((b, s, d), F32),
        grid=(b // bb, s // ts),
        in_specs=[tok(d), pl.BlockSpec((bb, 1, 3 * d), lambda i, j: (i, 0, 0)),
                  attn_out(oa), attn_out(ob), tok(WIDTH_A), tok(WIDTH_B), tok(d), tok(d),
                  const(wa16), const(wb16), const(wo16), const(fg)],
        out_specs=tok(d),
        compiler_params=_params(2),
        name="merge",
    )(x, mod, oa, ob, sza, szb, sga, sgb, wa16, wb16, wo16, fg)


def _rope_tables(pos):
    half = HEAD_DIM_A // 2
    inv = ROPE_THETA ** (-jnp.arange(half, dtype=F32) / half)
    ang = pos.astype(F32)[:, None] * inv[None, :]
    cos, sin = jnp.cos(ang), jnp.sin(ang)
    reps = LANES // HEAD_DIM_A
    return (jnp.tile(jnp.concatenate([cos, cos], axis=1), (1, reps)),
            jnp.tile(jnp.concatenate([-sin, sin], axis=1), (1, reps)))


def kernel(x_prompt, x_sample, c_prompt, c_sample, cache_a_k, cache_a_v, cache_b_k, cache_b_v, norm_g, w_ada, b_ada, w_in, lambda_q1, lambda_k1, lambda_q2, lambda_k2, subln_g, w_branch_a, w_branch_b, w_out, final_g):
    depth = w_in.shape[0]
    bp, sp, d = x_prompt.shape
    bs, ns, _ = x_sample.shape
    past = cache_a_k.shape[2]
    assert sp % PROJ_TOKENS == 0 and PROJ_TOKENS == DIFF_TILE and sp % SB_TILE == 0 and past % CACHE_BLOCK == 0
    assert ns % 16 == 0 and subln_g.shape[1] == LANES

    cos_p, sin_p = _rope_tables(jnp.arange(sp, dtype=jnp.int32))
    cos_s, sin_s = _rope_tables(past + jnp.arange(ns, dtype=jnp.int32))
    c_all = jnp.concatenate([c_prompt, c_sample], axis=0)

    xp, xs = x_prompt, x_sample
    caches = [[], [], [], [], [], [], [], []]
    for l in range(depth):
        lam_init = 0.8 - 0.6 * math.exp(-0.3 * l)
        final = l == depth - 1
        mod = _ada(c_all, w_ada[l], b_ada[l])
        mod_p = mod[:bp].reshape(bp, 1, 3 * d)
        mod_s = mod[bp:].reshape(bs, 1, 3 * d)
        w16 = w_in[l].astype(BF16)
        wa16, wb16, wo16 = (w[l].astype(BF16) for w in (w_branch_a, w_branch_b, w_out))
        lamv = jnp.stack([lambda_q1[l], lambda_k1[l], lambda_q2[l], lambda_k2[l]])

        (ka32, va32, kb32, vb32, qa, ka, va, qb, kb, vb, sza, szb, sga, sgb) = _proj(
            xp, mod_p, norm_g[l], cos_p, sin_p, w16, 1, PROJ_TOKENS, True)
        oa = _diff_attn(lamv, qa, ka, va, subln_g[l], lam_init)
        ob = _sb_attn(qb, kb, vb)
        xp = _merge(xp, mod_p, oa, ob, sza, szb, sga, sgb, wa16, wb16, wo16, final_g, 1, PROJ_TOKENS, final)
        for dst, val in zip(caches[:4], (ka32, va32, kb32, vb32)):
            dst.append(val)

        (ka32, va32, kb32, vb32, qa, ka, va, qb, kb, vb, sza, szb, sga, sgb) = _proj(
            xs, mod_s, norm_g[l], cos_s, sin_s, w16, bs, ns, False)
        oa = _diff_sample(lamv, qa, ka, va, cache_a_k[l].reshape(bs, past, WIDTH_A),
                          cache_a_v[l], subln_g[l], lam_init)
        ob = _sb_sample(qb, kb, vb, cache_b_k[l].reshape(bs, past, WIDTH_B),
                        cache_b_v[l].reshape(bs, past, WIDTH_B))
        xs = _merge(xs, mod_s, oa, ob, sza, szb, sga, sgb, wa16, wb16, wo16, final_g, bs, ns, final)
        for dst, val in zip(caches[4:], (ka32, va32, kb32, vb32)):
            dst.append(val)

    def stacked(vals, shape_tail):
        arr = vals[0][None] if depth == 1 else jnp.stack(vals)
        return arr.reshape(arr.shape[:3] + shape_tail)

    tails = [(N_HEADS_A, 2, HEAD_DIM_A), (N_HEADS_A, 2 * HEAD_DIM_A),
             (N_HEADS_B, HEAD_DIM_B), (N_HEADS_B, HEAD_DIM_B)] * 2
    return (xp, xs) + tuple(stacked(v, t) for v, t in zip(caches, tails))
```

```python
import functools
import math

import jax
import jax.numpy as jnp
from jax import lax
from jax.experimental import pallas as pl
from jax.experimental.pallas import tpu as pltpu

F32 = jnp.float32
BF16 = jnp.bfloat16

CHUNK = 64
N_HEADS_A = 4
HEAD_DIM_A = 64
WIDTH_A = N_HEADS_A * 2 * HEAD_DIM_A
N_HEADS_B = 8
HEAD_DIM_B = 64
WIDTH_B = N_HEADS_B * HEAD_DIM_B
ROPE_THETA = 10000.0
NORM_EPS = 1e-6

LANES = 128
HALF = 64
VMEM_LIMIT = 56 * 1024 * 1024
NEG = -1e30
SB_DONE = 150.0

PROJ_TOKENS = 512
DIFF_TILE = 512
SB_TILE = 256
SB_NEAR_BLOCKS = 2
LOG2E = math.log2(math.e)
CACHE_BLOCK = 256

_NT = (((1,), (1,)), ((), ()))


def _params(n_axes):
    return pltpu.CompilerParams(dimension_semantics=("parallel",) * n_axes,
                                vmem_limit_bytes=VMEM_LIMIT)


def _lane_halves(x):
    lane = lax.broadcasted_iota(jnp.int32, x.shape, x.ndim - 1)
    zero = jnp.zeros_like(x)
    return jnp.where(lane < HALF, x, zero), jnp.where(lane >= HALF, x, zero)


def _softplus2(z):
    return jnp.maximum(z, 0.0) + jnp.log2(1.0 + jnp.exp2(jnp.minimum(z, -z)))


def _suffix_matrix(n):
    j = lax.broadcasted_iota(jnp.int32, (n, n), 0)
    s = lax.broadcasted_iota(jnp.int32, (n, n), 1)
    return (j > s).astype(BF16)


def _ada_kernel(c_ref, w_ref, b_ref, o_ref):
    o_ref[...] = jnp.dot(c_ref[...], w_ref[...], preferred_element_type=F32,
                         precision=lax.Precision.HIGHEST) + b_ref[...]


def _ada(c, w, b):
    n, d = c.shape
    width = w.shape[1]
    tn = 512
    return pl.pallas_call(
        _ada_kernel,
        out_shape=jax.ShapeDtypeStruct((n, width), F32),
        grid=(width // tn,),
        in_specs=[pl.BlockSpec((n, d), lambda j: (0, 0)),
                  pl.BlockSpec((d, tn), lambda j: (0, j)),
                  pl.BlockSpec((1, tn), lambda j: (0, j))],
        out_specs=pl.BlockSpec((n, tn), lambda j: (0, j)),
        compiler_params=_params(1),
        name="ada",
    )(c, w, b.reshape(1, width))


def _rope(x, cos, sin_signed):
    lane = lax.broadcasted_iota(jnp.int32, x.shape, 1)
    first_half = (lane % HALF) < (HALF // 2)
    partner = jnp.where(first_half, pltpu.roll(x, LANES - HALF // 2, 1), pltpu.roll(x, HALF // 2, 1))
    return x * cos + partner * sin_signed


def _proj_kernel(x_ref, mod_ref, g_ref, cos_ref, sin_ref, w_ref,
                 ka32, va32, kb32, vb32, qa16, ka16, va16, qb16, kb16, vb16, sza, szb, sga, sgb):
    bb, ts, d = x_ref.shape
    n = bb * ts
    x = x_ref[...]
    y = x * lax.rsqrt(jnp.mean(x * x, axis=-1, keepdims=True) + NORM_EPS) * g_ref[...]
    mod = mod_ref[...]
    h = (y * (1.0 + mod[:, :, d:2 * d]) + mod[:, :, :d]).reshape(n, d).astype(BF16)
    cos = jnp.broadcast_to(cos_ref[...][None], (bb, ts, LANES)).reshape(n, LANES)
    sin = jnp.broadcast_to(sin_ref[...][None], (bb, ts, LANES)).reshape(n, LANES)

    def cols(start, width):
        return jnp.dot(h, w_ref[:, start:start + width], preferred_element_type=F32)

    def put(ref, val):
        if len(ref.shape) == 3:
            ref[...] = val.reshape(ref.shape).astype(ref.dtype)
            return
        src = val.T if len(ref.shape) == 5 else val
        for i in range(ref.shape[1]):
            if len(ref.shape) == 5:
                ref[0, i, 0] = src[i * LANES:(i + 1) * LANES].astype(ref.dtype)
            else:
                ref[0, i] = src[:, i * LANES:(i + 1) * LANES].astype(ref.dtype)

    def rope_all(val):
        return jnp.concatenate(
            [_rope(val[:, i * LANES:(i + 1) * LANES], cos, sin) for i in range(WIDTH_A // LANES)], axis=1)

    wa, wb = WIDTH_A, WIDTH_B
    put(qa16, rope_all(cols(0, wa)) * (HEAD_DIM_A ** -0.5 * LOG2E))
    ka = rope_all(cols(wa, wa))
    put(ka32, ka)
    put(ka16, ka)
    va = cols(2 * wa, wa)
    if len(va32.shape) == 4:
        for i in range(N_HEADS_A):
            va32[0, :, i, :] = va[:, i * LANES:(i + 1) * LANES]
    else:
        put(va32, va)
    put(va16, va)
    za = cols(3 * wa, wa)
    put(sza, za * jax.nn.sigmoid(za))
    off = 4 * wa
    put(qb16, cols(off, wb) * (HEAD_DIM_B ** -0.5 * LOG2E))
    kb = cols(off + wb, wb)
    put(kb32, kb)
    put(kb16, kb)
    vb = cols(off + 2 * wb, wb)
    put(vb32, vb)
    put(vb16, vb)
    zb = cols(off + 3 * wb, wb)
    put(szb, zb * jax.nn.sigmoid(zb))
    off += 4 * wb
    put(sga, jax.nn.sigmoid(cols(off, d)))
    put(sgb, jax.nn.sigmoid(cols(off + d, d)))


def _proj(x, mod, norm_g, cos, sin, w16, bb, ts, head_major):
    b, s, d = x.shape
    grid = (b // bb, s // ts)
    tok = lambda width: pl.BlockSpec((bb, ts, width), lambda i, j: (i, j, 0))
    shape = lambda width, dt: jax.ShapeDtypeStruct((b, s, width), dt)
    widths32 = [WIDTH_A, WIDTH_A, WIDTH_B, WIDTH_B]
    widths16 = [WIDTH_A] * 3 + [WIDTH_B] * 3 + [WIDTH_A, WIDTH_B, d, d]
    out_shape = [shape(w, F32) for w in widths32] + [shape(w, BF16) for w in widths16]
    out_specs = [tok(w) for w in widths32 + widths16]
    if head_major:
        assert bb == 1
        for o in range(4, 10):
            groups = widths16[o - 4] // LANES
            out_shape[o] = jax.ShapeDtypeStruct((b, groups, s, LANES), BF16)
            out_specs[o] = pl.BlockSpec((1, groups, ts, LANES), lambda i, j: (i, 0, j, 0))
        out_shape[1] = jax.ShapeDtypeStruct((b, s, N_HEADS_A, LANES), F32)
        out_specs[1] = pl.BlockSpec((1, ts, N_HEADS_A, LANES), lambda i, j: (i, j, 0, 0))
        out_shape[6] = jax.ShapeDtypeStruct((b, N_HEADS_A, s // ts, LANES, ts), BF16)
        out_specs[6] = pl.BlockSpec((1, N_HEADS_A, 1, LANES, ts), lambda i, j: (i, 0, j, 0, 0))
    return pl.pallas_call(
        _proj_kernel,
        out_shape=out_shape,
        grid=grid,
        in_specs=[tok(d),
                  pl.BlockSpec((bb, 1, 3 * d), lambda i, j: (i, 0, 0)),
                  pl.BlockSpec((1, 1, d), lambda i, j: (0, 0, 0)),
                  pl.BlockSpec((ts, LANES), lambda i, j: (j, 0)),
                  pl.BlockSpec((ts, LANES), lambda i, j: (j, 0)),
                  pl.BlockSpec(w16.shape, lambda i, j: (0, 0), pipeline_mode=pl.Buffered(1))],
        out_specs=out_specs,
        compiler_params=_params(2),
        name="proj",
    )(x, mod, norm_g.reshape(1, 1, d), cos, sin, w16)


def _diff_lambda(lam_ref, lam_init):
    lv = lam_ref[...]
    a = jnp.sum(lv[0:1] * lv[1:2], axis=1, keepdims=True)
    b = jnp.sum(lv[2:3] * lv[3:4], axis=1, keepdims=True)
    return jnp.exp(a) - jnp.exp(b) + lam_init


def _diff_finish(o0, o1, lam, g, lam_init):
    o = o0 - lam * o1
    on = o * lax.rsqrt(jnp.mean(o * o, axis=1, keepdims=True) + NORM_EPS) * g
    return on * (1.0 - lam_init)


def _diff_kernel(*refs, tile, lam_init):
    qi = pl.program_id(1)
    for h in range(N_HEADS_A):
        _diff_head(h, qi, *refs, tile=tile, lam_init=lam_init)


def _diff_head(h, qi, lam_ref, q_ref, k_ref, vt_ref, g_ref, o_ref,
               sa_ref, sb_ref, ta_ref, tb_ref, m_ref, l_ref, acc_ref, *, tile, lam_init):
    qc = _lane_halves(q_ref[0, h])
    buf_a, buf_b = (sa_ref, ta_ref), (sb_ref, tb_ref)
    m_ref[...] = jnp.full(m_ref.shape, NEG, F32)
    l_ref[...] = jnp.zeros(l_ref.shape, F32)
    acc_ref[...] = jnp.zeros(acc_ref.shape, F32)

    def produce(kb, buf):
        s_ref, top_ref = buf
        k = k_ref[0, h, pl.ds(pl.multiple_of(kb * tile, tile), tile), :]
        for c in range(2):
            s = lax.dot_general(k, qc[c], _NT, preferred_element_type=F32)
            s_ref[c] = s
            top_ref[c] = jnp.max(s, axis=0, keepdims=True)

    def consume(kb, buf):
        s_ref, top_ref = buf
        vt = vt_ref[0, h, kb]
        alpha, p = [], []
        for c in range(2):
            m_prev = m_ref[c]
            m_next = jnp.maximum(m_prev, top_ref[c])
            alpha.append(jnp.exp2(m_prev - m_next))
            p.append(jnp.exp2(s_ref[c] - m_next))
            l_ref[c] = alpha[c] * l_ref[c] + jnp.sum(p[c], axis=0, keepdims=True)
            m_ref[c] = m_next
        for c in range(2):
            acc_ref[c] = alpha[c] * acc_ref[c] + jnp.dot(vt, p[c].astype(BF16), preferred_element_type=F32)

    def two_blocks(i, carry):
        kb = 2 * i
        produce(kb + 1, buf_b)
        consume(kb, buf_a)
        produce(kb + 2, buf_a)
        consume(kb + 1, buf_b)
        return carry

    half = tile // 2

    def produce_diagonal(buf):
        s_ref, _ = buf
        k = k_ref[0, h, pl.ds(pl.multiple_of(qi * tile, tile), tile), :]
        for c in range(2):
            s_ref[c, :half, :] = lax.dot_general(k[:half], qc[c], _NT, preferred_element_type=F32)
            s_ref[c, half:, half:] = lax.dot_general(k[half:], qc[c][half:], _NT, preferred_element_type=F32)

    def consume_diagonal(buf):
        s_ref, _ = buf
        vt = vt_ref[0, h, qi]
        key = lax.broadcasted_iota(jnp.int32, (half, tile), 0)
        query = lax.broadcasted_iota(jnp.int32, (half, tile), 1)
        visible = (key // CHUNK) <= (query // CHUNK)
        for c in range(2):
            s0 = jnp.where(visible, s_ref[c, :half, :], NEG)
            s1 = jnp.where(visible[:, :half], s_ref[c, half:, half:], NEG)
            top0 = jnp.max(s0, axis=0, keepdims=True)
            top = jnp.concatenate(
                [top0[:, :half], jnp.maximum(top0[:, half:], jnp.max(s1, axis=0, keepdims=True))], axis=1)
            m_prev = m_ref[c]
            m_next = jnp.maximum(m_prev, top)
            alpha = jnp.exp2(m_prev - m_next)
            p0 = jnp.exp2(s0 - m_next)
            p1 = jnp.exp2(s1 - m_next[:, half:])
            sum0 = jnp.sum(p0, axis=0, keepdims=True)
            l_prev = alpha * l_ref[c]
            l_ref[c, :, :half] = l_prev[:, :half] + sum0[:, :half]
            l_ref[c, :, half:] = l_prev[:, half:] + sum0[:, half:] + jnp.sum(p1, axis=0, keepdims=True)
            pv0 = jnp.dot(vt[:, :half], p0.astype(BF16), preferred_element_type=F32)
            pv1 = jnp.dot(vt[:, half:], p1.astype(BF16), preferred_element_type=F32)
            acc_prev = alpha * acc_ref[c]
            acc_ref[c, :, :half] = acc_prev[:, :half] + pv0[:, :half]
            acc_ref[c, :, half:] = acc_prev[:, half:] + pv0[:, half:] + pv1
            m_ref[c] = m_next

    produce(0, buf_a)
    lax.fori_loop(0, qi // 2, two_blocks, 0)

    @pl.when(qi % 2 == 1)
    def _():
        produce_diagonal(buf_b)
        consume(qi - 1, buf_a)
        consume_diagonal(buf_b)

    @pl.when(qi % 2 == 0)
    def _():
        consume_diagonal(buf_a)

    lam = _diff_lambda(lam_ref, lam_init)
    o = acc_ref[0] * (1.0 / l_ref[0]) - acc_ref[1] * (lam / l_ref[1])
    on = o * lax.rsqrt(jnp.mean(o * o, axis=0, keepdims=True) + NORM_EPS) * g_ref[...]
    o_ref[0, h] = (on * (1.0 - lam_init)).T.astype(o_ref.dtype)


def _diff_attn(lamv, q, k, vt, subln_g, lam_init):
    b, heads, s, _ = q.shape
    t = DIFF_TILE
    q_spec = pl.BlockSpec((1, heads, t, LANES), lambda bi, qi: (bi, 0, qi, 0))
    return pl.pallas_call(
        functools.partial(_diff_kernel, tile=t, lam_init=lam_init),
        out_shape=jax.ShapeDtypeStruct(q.shape, BF16),
        grid=(b, s // t),
        in_specs=[pl.BlockSpec(lamv.shape, lambda bi, qi: (0, 0)),
                  q_spec,
                  pl.BlockSpec((1, heads, s, LANES), lambda bi, qi: (bi, 0, 0, 0)),
                  pl.BlockSpec((1, heads, s // t, LANES, t), lambda bi, qi: (bi, 0, 0, 0, 0)),
                  pl.BlockSpec((LANES, 1), lambda bi, qi: (0, 0))],
        out_specs=q_spec,
        scratch_shapes=[pltpu.VMEM((2, t, t), F32)] * 2
        + [pltpu.VMEM((2, 1, t), F32)] * 4 + [pltpu.VMEM((2, LANES, t), F32)],
        compiler_params=_params(2),
        name="diff_attn",
    )(lamv, q, k, vt, subln_g.reshape(LANES, 1))


def _sb_kernel(q_ref, k_ref, v_ref, u_ref, o_ref, cs_ref, acc_ref, *, tile):
    qi = pl.program_id(1)
    pairs = q_ref.shape[1]
    refs = (q_ref, k_ref, v_ref, u_ref, cs_ref, acc_ref)
    slowest = _sb_near_all(qi, *refs, tile=tile)

    @pl.when(jnp.logical_and(qi > SB_NEAR_BLOCKS, jnp.min(slowest) < SB_DONE))
    def _():
        def one_pair(hp, carry):
            _sb_far(hp, qi, *refs, tile=tile)
            return carry

        lax.fori_loop(0, pairs, one_pair, 0)

    for hp in range(pairs):
        o_ref[0, hp] = acc_ref[hp].astype(o_ref.dtype)


def _sb_near_all(qi, q_ref, k_ref, v_ref, u_ref, cs_ref, acc_ref, *, tile):
    pairs = q_ref.shape[1]
    lane = lax.broadcasted_iota(jnp.int32, (tile, LANES), 1)
    row = lax.broadcasted_iota(jnp.int32, (tile, tile), 0)
    col = lax.broadcasted_iota(jnp.int32, (tile, tile), 1)
    causal = col < row
    u = u_ref[...]
    blocks = [(qi, None)] + [(jnp.maximum(qi - j, 0), qi >= j) for j in range(1, SB_NEAR_BLOCKS + 1)]
    chains = [(hp, e, j) for hp in range(pairs) for e in range(2) for j in range(len(blocks))]

    z, values = {}, {}
    for hp in range(pairs):
        qe = _lane_halves(q_ref[0, hp])
        for j, (kb, exists) in enumerate(blocks):
            rows = pl.ds(pl.multiple_of(kb * tile, tile), tile)
            v = v_ref[0, hp, rows, :]
            values[hp, j] = v if exists is None else jnp.where(exists, v, jnp.zeros_like(v))
            for e in range(2):
                z[hp, e, j] = lax.dot_general(qe[e], k_ref[0, hp, rows, :], _NT, preferred_element_type=F32)

    sp, sums = {}, {}
    for c in chains:
        s = _softplus2(z[c])
        sp[c] = jnp.where(causal, s, 0.0) if c[2] == 0 else s
        total = jnp.broadcast_to(jnp.sum(sp[c], axis=1, keepdims=True), (tile, LANES))
        exists = blocks[c[2]][1]
        sums[c] = total if exists is None else jnp.where(exists, total, 0.0)

    rex = {c: jnp.dot(sp[c].astype(BF16), u, preferred_element_type=F32) for c in chains}

    weights, cs = {}, {}
    for hp in range(pairs):
        for e in range(2):
            after = jnp.zeros((tile, LANES), F32)
            for j in range(len(blocks)):
                c = (hp, e, j)
                a = jnp.exp2((z[c] - sp[c]) - rex[c] - jnp.tile(after, (1, tile // LANES)))
                weights[c] = jnp.where(causal, a, 0.0) if j == 0 else a
                after = after + sums[c]
            cs[hp, e] = after
            cs_ref[hp, e] = after

    slowest = None
    for hp in range(pairs):
        pv = [sum(jnp.dot(weights[hp, e, j].astype(BF16), values[hp, j], preferred_element_type=F32)
                  for j in range(len(blocks))) for e in range(2)]
        acc_ref[hp] = jnp.where(lane < HALF, pv[0], pv[1])
        pair_min = jnp.minimum(cs[hp, 0], cs[hp, 1])
        slowest = pair_min if slowest is None else jnp.minimum(slowest, pair_min)
    return slowest


def _sb_far(hp, qi, q_ref, k_ref, v_ref, u_ref, cs_ref, acc_ref, *, tile):
    qe = _lane_halves(q_ref[0, hp])
    lane = lax.broadcasted_iota(jnp.int32, (tile, LANES), 1)
    heads = range(2)

    def least_decay():
        return jnp.min(jnp.minimum(cs_ref[hp, 0], cs_ref[hp, 1]))

    def more(carry):
        kb, decay = carry
        return jnp.logical_and(kb >= 0, decay < SB_DONE)

    def full_block(carry):
        kb, _ = carry
        rows = pl.ds(pl.multiple_of(kb * tile, tile), tile)
        k, v = k_ref[0, hp, rows, :], v_ref[0, hp, rows, :]
        z = [lax.dot_general(qe[e], k, _NT, preferred_element_type=F32) for e in heads]
        sp = [_softplus2(z[e]) for e in heads]
        after = [cs_ref[hp, e] for e in heads]
        sums = [after[e] + jnp.sum(sp[e], axis=1, keepdims=True) for e in heads]
        rex = [jnp.dot(sp[e].astype(BF16), u_ref[...], preferred_element_type=F32) for e in heads]
        a = [jnp.exp2((z[e] - sp[e]) - rex[e] - jnp.tile(after[e], (1, tile // LANES))) for e in heads]
        pv = [jnp.dot(a[e].astype(BF16), v, preferred_element_type=F32) for e in heads]
        for e in heads:
            cs_ref[hp, e] = sums[e]
        acc_ref[hp] += jnp.where(lane < HALF, pv[0], pv[1])
        return kb - 1, jnp.min(jnp.minimum(sums[0], sums[1]))

    lax.while_loop(more, full_block, (qi - 1 - SB_NEAR_BLOCKS, least_decay()))


def _sb_attn(q, k, v):
    b, pairs, s, _ = q.shape
    t = SB_TILE
    kv_spec = pl.BlockSpec((1, pairs, s, LANES), lambda bi, qi: (bi, 0, 0, 0))
    q_spec = pl.BlockSpec((1, pairs, t, LANES), lambda bi, qi: (bi, 0, qi, 0))
    return pl.pallas_call(
        functools.partial(_sb_kernel, tile=t),
        out_shape=jax.ShapeDtypeStruct(q.shape, BF16),
        grid=(b, s // t),
        in_specs=[q_spec, kv_spec, kv_spec, pl.BlockSpec((t, t), lambda bi, qi: (0, 0))],
        out_specs=q_spec,
        scratch_shapes=[pltpu.VMEM((pairs, 2, t, LANES), F32), pltpu.VMEM((pairs, t, LANES), F32)],
        compiler_params=_params(2),
        name="sb_attn",
    )(q, k, v, _suffix_matrix(t))


def _diff_sample_kernel(lam_ref, q_ref, kn_ref, vn_ref, ck_ref, cv_ref, g_ref, o_ref, *, past, lam_init):
    ns = q_ref.shape[1]
    lam = _diff_lambda(lam_ref, lam_init)
    qpos = past + lax.broadcasted_iota(jnp.int32, (2 * ns, 1), 0) % ns
    cache_ok = (lax.broadcasted_iota(jnp.int32, (2 * ns, past), 1) // CHUNK) <= (qpos // CHUNK)
    new_ok = ((past + lax.broadcasted_iota(jnp.int32, (2 * ns, ns), 1)) // CHUNK) <= (qpos // CHUNK)
    heads = range(N_HEADS_A)
    lanes = [slice(h * LANES, (h + 1) * LANES) for h in heads]
    s_c, s_n = [], []
    for h in heads:
        qq = jnp.concatenate(_lane_halves(q_ref[0, :, lanes[h]]), axis=0)
        kc = ck_ref[0, :, lanes[h]].astype(BF16)
        s_c.append(jnp.where(cache_ok, lax.dot_general(qq, kc, _NT, preferred_element_type=F32), NEG))
        s_n.append(jnp.where(new_ok, lax.dot_general(qq, kn_ref[0, :, lanes[h]], _NT,
                                                     preferred_element_type=F32), NEG))
    m = [jnp.maximum(jnp.max(s_c[h], axis=1, keepdims=True), jnp.max(s_n[h], axis=1, keepdims=True))
         for h in heads]
    p_c = [jnp.exp2(s_c[h] - m[h]) for h in heads]
    p_n = [jnp.exp2(s_n[h] - m[h]) for h in heads]
    for h in heads:
        l = jnp.sum(p_c[h], axis=1, keepdims=True) + jnp.sum(p_n[h], axis=1, keepdims=True)
        vc = cv_ref[0, :, h, :].astype(BF16)
        acc = (jnp.dot(p_c[h].astype(BF16), vc, preferred_element_type=F32)
               + jnp.dot(p_n[h].astype(BF16), vn_ref[0, :, lanes[h]], preferred_element_type=F32))
        o = acc / l
        o_ref[0, :, lanes[h]] = _diff_finish(o[:ns], o[ns:], lam, g_ref[...], lam_init).astype(o_ref.dtype)


def _diff_sample(lamv, q, kn, vn, ck, cv, subln_g, lam_init):
    b, ns, _ = q.shape
    past = ck.shape[1]
    new = pl.BlockSpec((1, ns, WIDTH_A), lambda i: (i, 0, 0))
    cache = pl.BlockSpec((1, past, WIDTH_A), lambda i: (i, 0, 0))
    return pl.pallas_call(
        functools.partial(_diff_sample_kernel, past=past, lam_init=lam_init),
        out_shape=jax.ShapeDtypeStruct((b, ns, WIDTH_A), BF16),
        grid=(b,),
        in_specs=[pl.BlockSpec(lamv.shape, lambda i: (0, 0)), new, new, new, cache,
                  pl.BlockSpec((1, past, N_HEADS_A, LANES), lambda i: (i, 0, 0, 0)),
                  pl.BlockSpec((1, LANES), lambda i: (0, 0))],
        out_specs=new,
        compiler_params=_params(1),
        name="diff_sample",
    )(lamv, q, kn, vn, ck, cv, subln_g.reshape(1, LANES))


def _sb_sample_kernel(q_ref, kn_ref, vn_ref, ck_ref, cv_ref, un_ref, uc_ref, o_ref, *, past):
    ns = q_ref.shape[1]
    blk = CACHE_BLOCK
    t = lax.broadcasted_iota(jnp.int32, (2 * ns, ns), 0) % ns
    new_ok = lax.broadcasted_iota(jnp.int32, (2 * ns, ns), 1) < t
    lane = lax.broadcasted_iota(jnp.int32, (ns, LANES), 1)
    pairs = range(WIDTH_B // LANES)
    lanes = [slice(hp * LANES, (hp + 1) * LANES) for hp in pairs]
    keys = [None] + [slice(kb * blk, (kb + 1) * blk) for kb in reversed(range(past // blk))]
    chains = [(hp, j) for hp in pairs for j in range(len(keys))]

    z = {}
    for hp in pairs:
        qq = jnp.concatenate(_lane_halves(q_ref[0, :, lanes[hp]]), axis=0)
        for j, ks in enumerate(keys):
            k = kn_ref[0, :, lanes[hp]] if ks is None else ck_ref[0, ks, lanes[hp]].astype(BF16)
            z[hp, j] = lax.dot_general(qq, k, _NT, preferred_element_type=F32)
    sp = {c: jnp.where(new_ok, _softplus2(z[c]), 0.0) if c[1] == 0 else _softplus2(z[c]) for c in chains}
    rex = {c: jnp.dot(sp[c].astype(BF16), (un_ref if c[1] == 0 else uc_ref)[...], preferred_element_type=F32)
           for c in chains}
    for hp in pairs:
        cs = jnp.zeros((2 * ns, 1), F32)
        acc = jnp.zeros((2 * ns, LANES), F32)
        for j, ks in enumerate(keys):
            c = (hp, j)
            a = jnp.exp2((z[c] - sp[c]) - rex[c] - cs)
            if j == 0:
                a = jnp.where(new_ok, a, 0.0)
            v = vn_ref[0, :, lanes[hp]] if ks is None else cv_ref[0, ks, lanes[hp]].astype(BF16)
            acc += jnp.dot(a.astype(BF16), v, preferred_element_type=F32)
            cs = cs + jnp.sum(sp[c], axis=1, keepdims=True)
        o_ref[0, :, lanes[hp]] = jnp.where(lane < HALF, acc[:ns], acc[ns:]).astype(o_ref.dtype)


def _sb_sample(q, kn, vn, ck, cv):
    b, ns, _ = q.shape
    past = ck.shape[1]
    new = pl.BlockSpec((1, ns, WIDTH_B), lambda i: (i, 0, 0))
    cache = pl.BlockSpec((1, past, WIDTH_B), lambda i: (i, 0, 0))
    const = lambda n: pl.BlockSpec((n, n), lambda i: (0, 0))
    return pl.pallas_call(
        functools.partial(_sb_sample_kernel, past=past),
        out_shape=jax.ShapeDtypeStruct((b, ns, WIDTH_B), BF16),
        grid=(b,),
        in_specs=[new, new, new, cache, cache, const(ns), const(CACHE_BLOCK)],
        out_specs=new,
        compiler_params=_params(1),
        name="sb_sample",
    )(q, kn, vn, ck, cv, _suffix_matrix(ns), _suffix_matrix(CACHE_BLOCK))


def _merge_kernel(x_ref, mod_ref, oa_ref, ob_ref, sza_ref, szb_ref, sga_ref, sgb_ref,
                  wa_ref, wb_ref, wo_ref, fg_ref, y_ref, *, final_norm):
    bb, ts, d = x_ref.shape
    n = bb * ts

    def flat(ref):
        if len(ref.shape) == 4:
            return jnp.concatenate([ref[0, i] for i in range(ref.shape[1])], axis=1)
        return ref[...].reshape(n, ref.shape[-1])

    ya =jnp.dot(flat(oa_ref) * flat(sza_ref), wa_ref[...], preferred_element_type=F32)
    yb = jnp.dot(flat(ob_ref) * flat(szb_ref), wb_ref[...], preferred_element_type=F32)
    mixed = flat(sga_ref).astype(F32) * ya + flat(sgb_ref).astype(F32) * yb
    out = jnp.dot(mixed.astype(BF16), wo_ref[...], preferred_element_type=F32)
    xn = x_ref[...] + mod_ref[...][:, :, 2 * d:] * out.reshape(bb, ts, d)
    if final_norm:
        xn = xn * lax.rsqrt(jnp.mean(xn * xn, axis=-1, keepdims=True) + NORM_EPS) * fg_ref[...]
    y_ref[...] = xn


def _merge(x, mod, oa, ob, sza, szb, sga, sgb, wa16, wb16, wo16, final_g, bb, ts, final_norm):
    b, s, d = x.shape
    tok = lambda width: pl.BlockSpec((bb, ts, width), lambda i, j: (i, j, 0))
    const = lambda arr: pl.BlockSpec(arr.shape, lambda i, j: (0,) * arr.ndim)

    def attn_out(arr):
        if arr.ndim == 4:
            return pl.BlockSpec((1, arr.shape[1], ts, LANES), lambda i, j: (i, 0, j, 0))
        return tok(arr.shape[-1])

    fg = final_g.reshape(1, 1, d)
    return pl.pallas_call(
        functools.partial(_merge_kernel, final_norm=final_norm),
        out_shape=jax.ShapeDtypeStruct((b, s, d), F32),
        grid=(b // bb, s // ts),
        in_specs=[tok(d), pl.BlockSpec((bb, 1, 3 * d), lambda i, j: (i, 0, 0)),
                  attn_out(oa), attn_out(ob), tok(WIDTH_A), tok(WIDTH_B), tok(d), tok(d),
                  const(wa16), const(wb16), const(wo16), const(fg)],
        out_specs=tok(d),
        compiler_params=_params(2),
        name="merge",
    )(x, mod, oa, ob, sza, szb, sga, sgb, wa16, wb16, wo16, fg)


def _rope_tables(pos):
    half = HEAD_DIM_A // 2
    inv = ROPE_THETA ** (-jnp.arange(half, dtype=F32) / half)
    ang = pos.astype(F32)[:, None] * inv[None, :]
    cos, sin = jnp.cos(ang), jnp.sin(ang)
    reps = LANES // HEAD_DIM_A
    return (jnp.tile(jnp.concatenate([cos, cos], axis=1), (1, reps)),
            jnp.tile(jnp.concatenate([-sin, sin], axis=1), (1, reps)))


def kernel(x_prompt, x_sample, c_prompt, c_sample, cache_a_k, cache_a_v, cache_b_k, cache_b_v, norm_g, w_ada, b_ada, w_in, lambda_q1, lambda_k1, lambda_q2, lambda_k2, subln_g, w_branch_a, w_branch_b, w_out, final_g):
    depth = w_in.shape[0]
    bp, sp, d = x_prompt.shape
    bs, ns, _ = x_sample.shape
    past = cache_a_k.shape[2]
    assert sp % PROJ_TOKENS == 0 and PROJ_TOKENS == DIFF_TILE and sp % SB_TILE == 0 and past % CACHE_BLOCK == 0
    assert ns % 16 == 0 and subln_g.shape[1] == LANES

    cos_p, sin_p = _rope_tables(jnp.arange(sp, dtype=jnp.int32))
    cos_s, sin_s = _rope_tables(past + jnp.arange(ns, dtype=jnp.int32))
    c_all = jnp.concatenate([c_prompt, c_sample], axis=0)

    xp, xs = x_prompt, x_sample
    caches = [[], [], [], [], [], [], [], []]
    for l in range(depth):
        lam_init = 0.8 - 0.6 * math.exp(-0.3 * l)
        final = l == depth - 1
        mod = _ada(c_all, w_ada[l], b_ada[l])
        mod_p = mod[:bp].reshape(bp, 1, 3 * d)
        mod_s = mod[bp:].reshape(bs, 1, 3 * d)
        w16 = w_in[l].astype(BF16)
        wa16, wb16, wo16 = (w[l].astype(BF16) for w in (w_branch_a, w_branch_b, w_out))
        lamv = jnp.stack([lambda_q1[l], lambda_k1[l], lambda_q2[l], lambda_k2[l]])

        (ka32, va32, kb32, vb32, qa, ka, va, qb, kb, vb, sza, szb, sga, sgb) = _proj(
            xp, mod_p, norm_g[l], cos_p, sin_p, w16, 1, PROJ_TOKENS, True)
        oa = _diff_attn(lamv, qa, ka, va, subln_g[l], lam_init)
        ob = _sb_attn(qb, kb, vb)
        xp = _merge(xp, mod_p, oa, ob, sza, szb, sga, sgb, wa16, wb16, wo16, final_g, 1, PROJ_TOKENS, final)
        for dst, val in zip(caches[:4], (ka32, va32, kb32, vb32)):
            dst.append(val)

        (ka32, va32, kb32, vb32, qa, ka, va, qb, kb, vb, sza, szb, sga, sgb) = _proj(
            xs, mod_s, norm_g[l], cos_s, sin_s, w16, bs, ns, False)
        oa = _diff_sample(lamv, qa, ka, va, cache_a_k[l].reshape(bs, past, WIDTH_A),
                          cache_a_v[l], subln_g[l], lam_init)
        ob = _sb_sample(qb, kb, vb, cache_b_k[l].reshape(bs, past, WIDTH_B),
                        cache_b_v[l].reshape(bs, past, WIDTH_B))
        xs = _merge(xs, mod_s, oa, ob, sza, szb, sga, sgb, wa16, wb16, wo16, final_g, bs, ns, final)
        for dst, val in zip(caches[4:], (ka32, va32, kb32, vb32)):
            dst.append(val)

    def stacked(vals, shape_tail):
        arr = vals[0][None] if depth == 1 else jnp.stack(vals)
        return arr.reshape(arr.shape[:3] + shape_tail)

    tails = [(N_HEADS_A, 2, HEAD_DIM_A), (N_HEADS_A, 2 * HEAD_DIM_A),
             (N_HEADS_B, HEAD_DIM_B), (N_HEADS_B, HEAD_DIM_B)] * 2
    return (xp, xs) + tuple(stacked(v, t) for v, t in zip(caches, tails))
```

```python
import functools
import math

import jax
import jax.numpy as jnp
from jax import lax
from jax.experimental import pallas as pl
from jax.experimental.pallas import tpu as pltpu

F32 = jnp.float32
BF16 = jnp.bfloat16

CHUNK = 64
N_HEADS_A = 4
HEAD_DIM_A = 64
WIDTH_A = N_HEADS_A * 2 * HEAD_DIM_A
N_HEADS_B = 8
HEAD_DIM_B = 64
WIDTH_B = N_HEADS_B * HEAD_DIM_B
ROPE_THETA = 10000.0
NORM_EPS = 1e-6

LANES = 128
HALF = 64
VMEM_LIMIT = 56 * 1024 * 1024
NEG = -1e30
SB_DONE = 150.0

PROJ_TOKENS = 512
DIFF_TILE = 512
SB_TILE = 256
SB_NEAR_BLOCKS = 2
LOG2E = math.log2(math.e)
CACHE_BLOCK = 256

_NT = (((1,), (1,)), ((), ()))


def _params(n_axes):
    return pltpu.CompilerParams(dimension_semantics=("parallel",) * n_axes,
                                vmem_limit_bytes=VMEM_LIMIT)


def _lane_halves(x):
    lane = lax.broadcasted_iota(jnp.int32, x.shape, x.ndim - 1)
    zero = jnp.zeros_like(x)
    return jnp.where(lane < HALF, x, zero), jnp.where(lane >= HALF, x, zero)


def _softplus2(z):
    return jnp.maximum(z, 0.0) + jnp.log2(1.0 + jnp.exp2(jnp.minimum(z, -z)))


def _suffix_matrix(n):
    j = lax.broadcasted_iota(jnp.int32, (n, n), 0)
    s = lax.broadcasted_iota(jnp.int32, (n, n), 1)
    return (j > s).astype(BF16)


def _ada_kernel(c_ref, w_ref, b_ref, o_ref):
    o_ref[...] = jnp.dot(c_ref[...], w_ref[...], preferred_element_type=F32,
                         precision=lax.Precision.HIGHEST) + b_ref[...]


def _ada(c, w, b):
    n, d = c.shape
    width = w.shape[1]
    tn = 512
    return pl.pallas_call(
        _ada_kernel,
        out_shape=jax.ShapeDtypeStruct((n, width), F32),
        grid=(width // tn,),
        in_specs=[pl.BlockSpec((n, d), lambda j: (0, 0)),
                  pl.BlockSpec((d, tn), lambda j: (0, j)),
                  pl.BlockSpec((1, tn), lambda j: (0, j))],
        out_specs=pl.BlockSpec((n, tn), lambda j: (0, j)),
        compiler_params=_params(1),
        name="ada",
    )(c, w, b.reshape(1, width))


def _rope(x, cos, sin_signed):
    lane = lax.broadcasted_iota(jnp.int32, x.shape, 1)
    first_half = (lane % HALF) < (HALF // 2)
    partner = jnp.where(first_half, pltpu.roll(x, LANES - HALF // 2, 1), pltpu.roll(x, HALF // 2, 1))
    return x * cos + partner * sin_signed


def _proj_kernel(x_ref, mod_ref, g_ref, cos_ref, sin_ref, w_ref,
                 ka32, va32, kb32, vb32, qa16, ka16, va16, qb16, kb16, vb16, sza, szb, sga, sgb):
    bb, ts, d = x_ref.shape
    n = bb * ts
    x = x_ref[...]
    y = x * lax.rsqrt(jnp.mean(x * x, axis=-1, keepdims=True) + NORM_EPS) * g_ref[...]
    mod = mod_ref[...]
    h = (y * (1.0 + mod[:, :, d:2 * d]) + mod[:, :, :d]).reshape(n, d).astype(BF16)
    cos = jnp.broadcast_to(cos_ref[...][None], (bb, ts, LANES)).reshape(n, LANES)
    sin = jnp.broadcast_to(sin_ref[...][None], (bb, ts, LANES)).reshape(n, LANES)

    def cols(start, width):
        return jnp.dot(h, w_ref[:, start:start + width], preferred_element_type=F32)

    def put(ref, val):
        if len(ref.shape) == 3:
            ref[...] = val.reshape(ref.shape).astype(ref.dtype)
            return
        src = val.T if len(ref.shape) == 5 else val
        for i in range(ref.shape[1]):
            if len(ref.shape) == 5:
                ref[0, i, 0] = src[i * LANES:(i + 1) * LANES].astype(ref.dtype)
            else:
                ref[0, i] = src[:, i * LANES:(i + 1) * LANES].astype(ref.dtype)

    def rope_all(val):
        return jnp.concatenate(
            [_rope(val[:, i * LANES:(i + 1) * LANES], cos, sin) for i in range(WIDTH_A // LANES)], axis=1)

    wa, wb = WIDTH_A, WIDTH_B
    put(qa16, rope_all(cols(0, wa)) * (HEAD_DIM_A ** -0.5 * LOG2E))
    ka = rope_all(cols(wa, wa))
    put(ka32, ka)
    put(ka16, ka)
    va = cols(2 * wa, wa)
    if len(va32.shape) == 4:
        for i in range(N_HEADS_A):
            va32[0, :, i, :] = va[:, i * LANES:(i + 1) * LANES]
    else:
        put(va32, va)
    put(va16, va)
    za = cols(3 * wa, wa)
    put(sza, za * jax.nn.sigmoid(za))
    off = 4 * wa
    put(qb16, cols(off, wb) * (HEAD_DIM_B ** -0.5 * LOG2E))
    kb = cols(off + wb, wb)
    put(kb32, kb)
    put(kb16, kb)
    vb = cols(off + 2 * wb, wb)
    put(vb32, vb)
    put(vb16, vb)
    zb = cols(off + 3 * wb, wb)
    put(szb, zb * jax.nn.sigmoid(zb))
    off += 4 * wb
    put(sga, jax.nn.sigmoid(cols(off, d)))
    put(sgb, jax.nn.sigmoid(cols(off + d, d)))


def _proj(x, mod, norm_g, cos, sin, w16, bb, ts, head_major):
    b, s, d = x.shape
    grid = (b // bb, s // ts)
    tok = lambda width: pl.BlockSpec((bb, ts, width), lambda i, j: (i, j, 0))
    shape = lambda width, dt: jax.ShapeDtypeStruct((b, s, width), dt)
    widths32 = [WIDTH_A, WIDTH_A, WIDTH_B, WIDTH_B]
    widths16 = [WIDTH_A] * 3 + [WIDTH_B] * 3 + [WIDTH_A, WIDTH_B, d, d]
    out_shape = [shape(w, F32) for w in widths32] + [shape(w, BF16) for w in widths16]
    out_specs = [tok(w) for w in widths32 + widths16]
    if head_major:
        assert bb == 1
        for o in range(4, 10):
            groups = widths16[o - 4] // LANES
            out_shape[o] = jax.ShapeDtypeStruct((b, groups, s, LANES), BF16)
            out_specs[o] = pl.BlockSpec((1, groups, ts, LANES), lambda i, j: (i, 0, j, 0))
        out_shape[1] = jax.ShapeDtypeStruct((b, s, N_HEADS_A, LANES), F32)
        out_specs[1] = pl.BlockSpec((1, ts, N_HEADS_A, LANES), lambda i, j: (i, j, 0, 0))
        for o in (4, 6):
            out_shape[o] = jax.ShapeDtypeStruct((b, N_HEADS_A, s // ts, LANES, ts), BF16)
            out_specs[o] = pl.BlockSpec((1, N_HEADS_A, 1, LANES, ts), lambda i, j: (i, 0, j, 0, 0))
    return pl.pallas_call(
        _proj_kernel,
        out_shape=out_shape,
        grid=grid,
        in_specs=[tok(d),
                  pl.BlockSpec((bb, 1, 3 * d), lambda i, j: (i, 0, 0)),
                  pl.BlockSpec((1, 1, d), lambda i, j: (0, 0, 0)),
                  pl.BlockSpec((ts, LANES), lambda i, j: (j, 0)),
                  pl.BlockSpec((ts, LANES), lambda i, j: (j, 0)),
                  pl.BlockSpec(w16.shape, lambda i, j: (0, 0), pipeline_mode=pl.Buffered(1))],
        out_specs=out_specs,
        compiler_params=_params(2),
        name="proj",
    )(x, mod, norm_g.reshape(1, 1, d), cos, sin, w16)


def _diff_lambda(lam_ref, lam_init):
    lv = lam_ref[...]
    a = jnp.sum(lv[0:1] * lv[1:2], axis=1, keepdims=True)
    b = jnp.sum(lv[2:3] * lv[3:4], axis=1, keepdims=True)
    return jnp.exp(a) - jnp.exp(b) + lam_init


def _diff_finish(o0, o1, lam, g, lam_init):
    o = o0 - lam * o1
    on = o * lax.rsqrt(jnp.mean(o * o, axis=1, keepdims=True) + NORM_EPS) * g
    return on * (1.0 - lam_init)


def _diff_kernel(*refs, tile, lam_init):
    qi = pl.program_id(1)
    for h in range(N_HEADS_A):
        _diff_head(h, qi, *refs, tile=tile, lam_init=lam_init)


def _diff_head(h, qi, lam_ref, q_ref, k_ref, vt_ref, g_ref, o_ref,
               sa_ref, sb_ref, ta_ref, tb_ref, m_ref, l_ref, acc_ref, *, tile, lam_init):
    qt = q_ref[0, h, 0]
    feature = lax.broadcasted_iota(jnp.int32, qt.shape, 0)
    qc = (jnp.where(feature < HALF, qt, jnp.zeros_like(qt)), jnp.where(feature >= HALF, qt, jnp.zeros_like(qt)))
    buf_a, buf_b = (sa_ref, ta_ref), (sb_ref, tb_ref)
    m_ref[...] = jnp.full(m_ref.shape, NEG, F32)
    l_ref[...] = jnp.zeros(l_ref.shape, F32)
    acc_ref[...] = jnp.zeros(acc_ref.shape, F32)

    def produce(kb, buf):
        s_ref, top_ref = buf
        k = k_ref[0, h, pl.ds(pl.multiple_of(kb * tile, tile), tile), :]
        for c in range(2):
            s = jnp.dot(k, qc[c], preferred_element_type=F32)
            s_ref[c] = s
            top_ref[c] = jnp.max(s, axis=0, keepdims=True)

    def consume(kb, buf):
        s_ref, top_ref = buf
        vt = vt_ref[0, h, kb]
        alpha, p = [], []
        for c in range(2):
            m_prev = m_ref[c]
            m_next = jnp.maximum(m_prev, top_ref[c])
            alpha.append(jnp.exp2(m_prev - m_next))
            p.append(jnp.exp2(s_ref[c] - m_next))
            l_ref[c] = alpha[c] * l_ref[c] + jnp.sum(p[c], axis=0, keepdims=True)
            m_ref[c] = m_next
        for c in range(2):
            acc_ref[c] = alpha[c] * acc_ref[c] + jnp.dot(vt, p[c].astype(BF16), preferred_element_type=F32)

    def two_blocks(i, carry):
        kb = 2 * i
        produce(kb + 1, buf_b)
        consume(kb, buf_a)
        produce(kb + 2, buf_a)
        consume(kb + 1, buf_b)
        return carry

    half = tile // 2

    def produce_diagonal(buf):
        s_ref, _ = buf
        k = k_ref[0, h, pl.ds(pl.multiple_of(qi * tile, tile), tile), :]
        for c in range(2):
            s_ref[c, :half, :] = jnp.dot(k[:half], qc[c], preferred_element_type=F32)
            s_ref[c, half:, half:] = jnp.dot(k[half:], qc[c][:, half:], preferred_element_type=F32)

    def consume_diagonal(buf):
        s_ref, _ = buf
        vt = vt_ref[0, h, qi]
        key = lax.broadcasted_iota(jnp.int32, (half, tile), 0)
        query = lax.broadcasted_iota(jnp.int32, (half, tile), 1)
        visible = (key // CHUNK) <= (query // CHUNK)
        for c in range(2):
            s0 = jnp.where(visible, s_ref[c, :half, :], NEG)
            s1 = jnp.where(visible[:, :half], s_ref[c, half:, half:], NEG)
            top0 = jnp.max(s0, axis=0, keepdims=True)
            top = jnp.concatenate(
                [top0[:, :half], jnp.maximum(top0[:, half:], jnp.max(s1, axis=0, keepdims=True))], axis=1)
            m_prev = m_ref[c]
            m_next = jnp.maximum(m_prev, top)
            alpha = jnp.exp2(m_prev - m_next)
            p0 = jnp.exp2(s0 - m_next)
            p1 = jnp.exp2(s1 - m_next[:, half:])
            sum0 = jnp.sum(p0, axis=0, keepdims=True)
            l_prev = alpha * l_ref[c]
            l_ref[c, :, :half] = l_prev[:, :half] + sum0[:, :half]
            l_ref[c, :, half:] = l_prev[:, half:] + sum0[:, half:] + jnp.sum(p1, axis=0, keepdims=True)
            pv0 = jnp.dot(vt[:, :half], p0.astype(BF16), preferred_element_type=F32)
            pv1 = jnp.dot(vt[:, half:], p1.astype(BF16), preferred_element_type=F32)
            acc_prev = alpha * acc_ref[c]
            acc_ref[c, :, :half] = acc_prev[:, :half] + pv0[:, :half]
            acc_ref[c, :, half:] = acc_prev[:, half:] + pv0[:, half:] + pv1
            m_ref[c] = m_next

    produce(0, buf_a)
    lax.fori_loop(0, qi // 2, two_blocks, 0)

    @pl.when(qi % 2 == 1)
    def _():
        produce_diagonal(buf_b)
        consume(qi - 1, buf_a)
        consume_diagonal(buf_b)

    @pl.when(qi % 2 == 0)
    def _():
        consume_diagonal(buf_a)

    lam = _diff_lambda(lam_ref, lam_init)
    o = acc_ref[0] * (1.0 / l_ref[0]) - acc_ref[1] * (lam / l_ref[1])
    on = o * lax.rsqrt(jnp.mean(o * o, axis=0, keepdims=True) + NORM_EPS) * g_ref[...]
    o_ref[0, h] = (on * (1.0 - lam_init)).T.astype(o_ref.dtype)


def _diff_attn(lamv, q, k, vt, subln_g, lam_init):
    b, heads, s, _ = k.shape
    t = DIFF_TILE
    return pl.pallas_call(
        functools.partial(_diff_kernel, tile=t, lam_init=lam_init),
        out_shape=jax.ShapeDtypeStruct(k.shape, BF16),
        grid=(b, s // t),
        in_specs=[pl.BlockSpec(lamv.shape, lambda bi, qi: (0, 0)),
                  pl.BlockSpec((1, heads, 1, LANES, t), lambda bi, qi: (bi, 0, qi, 0, 0)),
                  pl.BlockSpec((1, heads, s, LANES), lambda bi, qi: (bi, 0, 0, 0)),
                  pl.BlockSpec((1, heads, s // t, LANES, t), lambda bi, qi: (bi, 0, 0, 0, 0)),
                  pl.BlockSpec((LANES, 1), lambda bi, qi: (0, 0))],
        out_specs=pl.BlockSpec((1, heads, t, LANES), lambda bi, qi: (bi, 0, qi, 0)),
        scratch_shapes=[pltpu.VMEM((2, t, t), F32)] * 2
        + [pltpu.VMEM((2, 1, t), F32)] * 4 + [pltpu.VMEM((2, LANES, t), F32)],
        compiler_params=_params(2),
        name="diff_attn",
    )(lamv, q, k, vt, subln_g.reshape(LANES, 1))


def _sb_kernel(q_ref, k_ref, v_ref, u_ref, o_ref, cs_ref, acc_ref, *, tile):
    qi = pl.program_id(1)
    pairs = q_ref.shape[1]
    refs = (q_ref, k_ref, v_ref, u_ref, cs_ref, acc_ref)
    slowest = _sb_near_all(qi, *refs, tile=tile)

    @pl.when(jnp.logical_and(qi > SB_NEAR_BLOCKS, jnp.min(slowest) < SB_DONE))
    def _():
        def one_pair(hp, carry):
            _sb_far(hp, qi, *refs, tile=tile)
            return carry

        lax.fori_loop(0, pairs, one_pair, 0)

    for hp in range(pairs):
        o_ref[0, hp] = acc_ref[hp].astype(o_ref.dtype)


def _sb_near_all(qi, q_ref, k_ref, v_ref, u_ref, cs_ref, acc_ref, *, tile):
    pairs = q_ref.shape[1]
    lane = lax.broadcasted_iota(jnp.int32, (tile, LANES), 1)
    row = lax.broadcasted_iota(jnp.int32, (tile, tile), 0)
    col = lax.broadcasted_iota(jnp.int32, (tile, tile), 1)
    causal = col < row
    u = u_ref[...]
    blocks = [(qi, None)] + [(jnp.maximum(qi - j, 0), qi >= j) for j in range(1, SB_NEAR_BLOCKS + 1)]
    chains = [(hp, e, j) for hp in range(pairs) for e in range(2) for j in range(len(blocks))]

    z, values = {}, {}
    for hp in range(pairs):
        qe = _lane_halves(q_ref[0, hp])
        for j, (kb, exists) in enumerate(blocks):
            rows = pl.ds(pl.multiple_of(kb * tile, tile), tile)
            v = v_ref[0, hp, rows, :]
            values[hp, j] = v if exists is None else jnp.where(exists, v, jnp.zeros_like(v))
            for e in range(2):
                z[hp, e, j] = lax.dot_general(qe[e], k_ref[0, hp, rows, :], _NT, preferred_element_type=F32)

    sp, sums = {}, {}
    for c in chains:
        s = _softplus2(z[c])
        sp[c] = jnp.where(causal, s, 0.0) if c[2] == 0 else s
        total = jnp.broadcast_to(jnp.sum(sp[c], axis=1, keepdims=True), (tile, LANES))
        exists = blocks[c[2]][1]
        sums[c] = total if exists is None else jnp.where(exists, total, 0.0)

    rex = {c: jnp.dot(sp[c].astype(BF16), u, preferred_element_type=F32) for c in chains}

    weights, cs = {}, {}
    for hp in range(pairs):
        for e in range(2):
            after = jnp.zeros((tile, LANES), F32)
            for j in range(len(blocks)):
                c = (hp, e, j)
                a = jnp.exp2((z[c] - sp[c]) - rex[c] - jnp.tile(after, (1, tile // LANES)))
                weights[c] = jnp.where(causal, a, 0.0) if j == 0 else a
                after = after + sums[c]
            cs[hp, e] = after
            cs_ref[hp, e] = after

    slowest = None
    for hp in range(pairs):
        pv = [sum(jnp.dot(weights[hp, e, j].astype(BF16), values[hp, j], preferred_element_type=F32)
                  for j in range(len(blocks))) for e in range(2)]
        acc_ref[hp] = jnp.where(lane < HALF, pv[0], pv[1])
        pair_min = jnp.minimum(cs[hp, 0], cs[hp, 1])
        slowest = pair_min if slowest is None else jnp.minimum(slowest, pair_min)
    return slowest


def _sb_far(hp, qi, q_ref, k_ref, v_ref, u_ref, cs_ref, acc_ref, *, tile):
    qe = _lane_halves(q_ref[0, hp])
    lane = lax.broadcasted_iota(jnp.int32, (tile, LANES), 1)
    heads = range(2)

    def least_decay():
        return jnp.min(jnp.minimum(cs_ref[hp, 0], cs_ref[hp, 1]))

    def more(carry):
        kb, decay = carry
        return jnp.logical_and(kb >= 0, decay < SB_DONE)

    def full_block(carry):
        kb, _ = carry
        rows = pl.ds(pl.multiple_of(kb * tile, tile), tile)
        k, v = k_ref[0, hp, rows, :], v_ref[0, hp, rows, :]
        z = [lax.dot_general(qe[e], k, _NT, preferred_element_type=F32) for e in heads]
        sp = [_softplus2(z[e]) for e in heads]
        after = [cs_ref[hp, e] for e in heads]
        sums = [after[e] + jnp.sum(sp[e], axis=1, keepdims=True) for e in heads]
        rex = [jnp.dot(sp[e].astype(BF16), u_ref[...], preferred_element_type=F32) for e in heads]
        a = [jnp.exp2((z[e] - sp[e]) - rex[e] - jnp.tile(after[e], (1, tile // LANES))) for e in heads]
        pv = [jnp.dot(a[e].astype(BF16), v, preferred_element_type=F32) for e in heads]
        for e in heads:
            cs_ref[hp, e] = sums[e]
        acc_ref[hp] += jnp.where(lane < HALF, pv[0], pv[1])
        return kb - 1, jnp.min(jnp.minimum(sums[0], sums[1]))

    lax.while_loop(more, full_block, (qi - 1 - SB_NEAR_BLOCKS, least_decay()))


def _sb_attn(q, k, v):
    b, pairs, s, _ = q.shape
    t = SB_TILE
    kv_spec = pl.BlockSpec((1, pairs, s, LANES), lambda bi, qi: (bi, 0, 0, 0))
    q_spec = pl.BlockSpec((1, pairs, t, LANES), lambda bi, qi: (bi, 0, qi, 0))
    return pl.pallas_call(
        functools.partial(_sb_kernel, tile=t),
        out_shape=jax.ShapeDtypeStruct(q.shape, BF16),
        grid=(b, s // t),
        in_specs=[q_spec, kv_spec, kv_spec, pl.BlockSpec((t, t), lambda bi, qi: (0, 0))],
        out_specs=q_spec,
        scratch_shapes=[pltpu.VMEM((pairs, 2, t, LANES), F32), pltpu.VMEM((pairs, t, LANES), F32)],
        compiler_params=_params(2),
        name="sb_attn",
    )(q, k, v, _suffix_matrix(t))


def _diff_sample_kernel(lam_ref, q_ref, kn_ref, vn_ref, ck_ref, cv_ref, g_ref, o_ref, *, past, lam_init):
    ns = q_ref.shape[1]
    lam = _diff_lambda(lam_ref, lam_init)
    qpos = past + lax.broadcasted_iota(jnp.int32, (2 * ns, 1), 0) % ns
    cache_ok = (lax.broadcasted_iota(jnp.int32, (2 * ns, past), 1) // CHUNK) <= (qpos // CHUNK)
    new_ok = ((past + lax.broadcasted_iota(jnp.int32, (2 * ns, ns), 1)) // CHUNK) <= (qpos // CHUNK)
    heads = range(N_HEADS_A)
    lanes = [slice(h * LANES, (h + 1) * LANES) for h in heads]
    s_c, s_n = [], []
    for h in heads:
        qq = jnp.concatenate(_lane_halves(q_ref[0, :, lanes[h]]), axis=0)
        kc = ck_ref[0, :, lanes[h]].astype(BF16)
        s_c.append(jnp.where(cache_ok, lax.dot_general(qq, kc, _NT, preferred_element_type=F32), NEG))
        s_n.append(jnp.where(new_ok, lax.dot_general(qq, kn_ref[0, :, lanes[h]], _NT,
                                                     preferred_element_type=F32), NEG))
    m = [jnp.maximum(jnp.max(s_c[h], axis=1, keepdims=True), jnp.max(s_n[h], axis=1, keepdims=True))
         for h in heads]
    p_c = [jnp.exp2(s_c[h] - m[h]) for h in heads]
    p_n = [jnp.exp2(s_n[h] - m[h]) for h in heads]
    for h in heads:
        l = jnp.sum(p_c[h], axis=1, keepdims=True) + jnp.sum(p_n[h], axis=1, keepdims=True)
        vc = cv_ref[0, :, h, :].astype(BF16)
        acc = (jnp.dot(p_c[h].astype(BF16), vc, preferred_element_type=F32)
               + jnp.dot(p_n[h].astype(BF16), vn_ref[0, :, lanes[h]], preferred_element_type=F32))
        o = acc / l
        o_ref[0, :, lanes[h]] = _diff_finish(o[:ns], o[ns:], lam, g_ref[...], lam_init).astype(o_ref.dtype)


def _diff_sample(lamv, q, kn, vn, ck, cv, subln_g, lam_init):
    b, ns, _ = q.shape
    past = ck.shape[1]
    new = pl.BlockSpec((1, ns, WIDTH_A), lambda i: (i, 0, 0))
    cache = pl.BlockSpec((1, past, WIDTH_A), lambda i: (i, 0, 0))
    return pl.pallas_call(
        functools.partial(_diff_sample_kernel, past=past, lam_init=lam_init),
        out_shape=jax.ShapeDtypeStruct((b, ns, WIDTH_A), BF16),
        grid=(b,),
        in_specs=[pl.BlockSpec(lamv.shape, lambda i: (0, 0)), new, new, new, cache,
                  pl.BlockSpec((1, past, N_HEADS_A, LANES), lambda i: (i, 0, 0, 0)),
                  pl.BlockSpec((1, LANES), lambda i: (0, 0))],
        out_specs=new,
        compiler_params=_params(1),
        name="diff_sample",
    )(lamv, q, kn, vn, ck, cv, subln_g.reshape(1, LANES))


def _sb_sample_kernel(q_ref, kn_ref, vn_ref, ck_ref, cv_ref, un_ref, uc_ref, o_ref, *, past):
    ns = q_ref.shape[1]
    blk = CACHE_BLOCK
    t = lax.broadcasted_iota(jnp.int32, (2 * ns, ns), 0) % ns
    new_ok = lax.broadcasted_iota(jnp.int32, (2 * ns, ns), 1) < t
    lane = lax.broadcasted_iota(jnp.int32, (ns, LANES), 1)
    pairs = range(WIDTH_B // LANES)
    lanes = [slice(hp * LANES, (hp + 1) * LANES) for hp in pairs]
    keys = [None] + [slice(kb * blk, (kb + 1) * blk) for kb in reversed(range(past // blk))]
    chains = [(hp, j) for hp in pairs for j in range(len(keys))]

    z = {}
    for hp in pairs:
        qq = jnp.concatenate(_lane_halves(q_ref[0, :, lanes[hp]]), axis=0)
        for j, ks in enumerate(keys):
            k = kn_ref[0, :, lanes[hp]] if ks is None else ck_ref[0, ks, lanes[hp]].astype(BF16)
            z[hp, j] = lax.dot_general(qq, k, _NT, preferred_element_type=F32)
    sp = {c: jnp.where(new_ok, _softplus2(z[c]), 0.0) if c[1] == 0 else _softplus2(z[c]) for c in chains}
    rex = {c: jnp.dot(sp[c].astype(BF16), (un_ref if c[1] == 0 else uc_ref)[...], preferred_element_type=F32)
           for c in chains}
    for hp in pairs:
        cs = jnp.zeros((2 * ns, 1), F32)
        acc = jnp.zeros((2 * ns, LANES), F32)
        for j, ks in enumerate(keys):
            c = (hp, j)
            a = jnp.exp2((z[c] - sp[c]) - rex[c] - cs)
            if j == 0:
                a = jnp.where(new_ok, a, 0.0)
            v = vn_ref[0, :, lanes[hp]] if ks is None else cv_ref[0, ks, lanes[hp]].astype(BF16)
            acc += jnp.dot(a.astype(BF16), v, preferred_element_type=F32)
            cs = cs + jnp.sum(sp[c], axis=1, keepdims=True)
        o_ref[0, :, lanes[hp]] = jnp.where(lane < HALF, acc[:ns], acc[ns:]).astype(o_ref.dtype)


def _sb_sample(q, kn, vn, ck, cv):
    b, ns, _ = q.shape
    past = ck.shape[1]
    new = pl.BlockSpec((1, ns, WIDTH_B), lambda i: (i, 0, 0))
    cache = pl.BlockSpec((1, past, WIDTH_B), lambda i: (i, 0, 0))
    const = lambda n: pl.BlockSpec((n, n), lambda i: (0, 0))
    return pl.pallas_call(
        functools.partial(_sb_sample_kernel, past=past),
        out_shape=jax.ShapeDtypeStruct((b, ns, WIDTH_B), BF16),
        grid=(b,),
        in_specs=[new, new, new, cache, cache, const(ns), const(CACHE_BLOCK)],
        out_specs=new,
        compiler_params=_params(1),
        name="sb_sample",
    )(q, kn, vn, ck, cv, _suffix_matrix(ns), _suffix_matrix(CACHE_BLOCK))


def _merge_kernel(x_ref, mod_ref, oa_ref, ob_ref, sza_ref, szb_ref, sga_ref, sgb_ref,
                  wa_ref, wb_ref, wo_ref, fg_ref, y_ref, *, final_norm):
    bb, ts, d = x_ref.shape
    n = bb * ts

    def flat(ref):
        if len(ref.shape) == 4:
            return jnp.concatenate([ref[0, i] for i in range(ref.shape[1])], axis=1)
        return ref[...].reshape(n, ref.shape[-1])

    ya =jnp.dot(flat(oa_ref) * flat(sza_ref), wa_ref[...], preferred_element_type=F32)
    yb = jnp.dot(flat(ob_ref) * flat(szb_ref), wb_ref[...], preferred_element_type=F32)
    mixed = flat(sga_ref).astype(F32) * ya + flat(sgb_ref).astype(F32) * yb
    out = jnp.dot(mixed.astype(BF16), wo_ref[...], preferred_element_type=F32)
    xn = x_ref[...] + mod_ref[...][:, :, 2 * d:] * out.reshape(bb, ts, d)
    if final_norm:
        xn = xn * lax.rsqrt(jnp.mean(xn * xn, axis=-1, keepdims=True) + NORM_EPS) * fg_ref[...]
    y_ref[...] = xn


def _merge(x, mod, oa, ob, sza, szb, sga, sgb, wa16, wb16, wo16, final_g, bb, ts, final_norm):
    b, s, d = x.shape
    tok = lambda width: pl.BlockSpec((bb, ts, width), lambda i, j: (i, j, 0))
    const = lambda arr: pl.BlockSpec(arr.shape, lambda i, j: (0,) * arr.ndim)

    def attn_out(arr):
        if arr.ndim == 4:
            return pl.BlockSpec((1, arr.shape[1], ts, LANES), lambda i, j: (i, 0, j, 0))
        return tok(arr.shape[-1])

    fg = final_g.reshape(1, 1, d)
    return pl.pallas_call(
        functools.partial(_merge_kernel, final_norm=final_norm),
        out_shape=jax.ShapeDtypeStruct((b, s, d), F32),
        grid=(b // bb, s // ts),
        in_specs=[tok(d), pl.BlockSpec((bb, 1, 3 * d), lambda i, j: (i, 0, 0)),
                  attn_out(oa), attn_out(ob), tok(WIDTH_A), tok(WIDTH_B), tok(d), tok(d),
                  const(wa16), const(wb16), const(wo16), const(fg)],
        out_specs=tok(d),
        compiler_params=_params(2),
        name="merge",
    )(x, mod, oa, ob, sza, szb, sga, sgb, wa16, wb16, wo16, fg)


def _rope_tables(pos):
    half = HEAD_DIM_A // 2
    inv = ROPE_THETA ** (-jnp.arange(half, dtype=F32) / half)
    ang = pos.astype(F32)[:, None] * inv[None, :]
    cos, sin = jnp.cos(ang), jnp.sin(ang)
    reps = LANES // HEAD_DIM_A
    return (jnp.tile(jnp.concatenate([cos, cos], axis=1), (1, reps)),
            jnp.tile(jnp.concatenate([-sin, sin], axis=1), (1, reps)))


def kernel(x_prompt, x_sample, c_prompt, c_sample, cache_a_k, cache_a_v, cache_b_k, cache_b_v, norm_g, w_ada, b_ada, w_in, lambda_q1, lambda_k1, lambda_q2, lambda_k2, subln_g, w_branch_a, w_branch_b, w_out, final_g):
    depth = w_in.shape[0]
    bp, sp, d = x_prompt.shape
    bs, ns, _ = x_sample.shape
    past = cache_a_k.shape[2]
    assert sp % PROJ_TOKENS == 0 and PROJ_TOKENS == DIFF_TILE and sp % SB_TILE == 0 and past % CACHE_BLOCK == 0
    assert ns % 16 == 0 and subln_g.shape[1] == LANES

    cos_p, sin_p = _rope_tables(jnp.arange(sp, dtype=jnp.int32))
    cos_s, sin_s = _rope_tables(past + jnp.arange(ns, dtype=jnp.int32))
    c_all = jnp.concatenate([c_prompt, c_sample], axis=0)

    xp, xs = x_prompt, x_sample
    caches = [[], [], [], [], [], [], [], []]
    for l in range(depth):
        lam_init = 0.8 - 0.6 * math.exp(-0.3 * l)
        final = l == depth - 1
        mod = _ada(c_all, w_ada[l], b_ada[l])
        mod_p = mod[:bp].reshape(bp, 1, 3 * d)
        mod_s = mod[bp:].reshape(bs, 1, 3 * d)
        w16 = w_in[l].astype(BF16)
        wa16, wb16, wo16 = (w[l].astype(BF16) for w in (w_branch_a, w_branch_b, w_out))
        lamv = jnp.stack([lambda_q1[l], lambda_k1[l], lambda_q2[l], lambda_k2[l]])

        (ka32, va32, kb32, vb32, qa, ka, va, qb, kb, vb, sza, szb, sga, sgb) = _proj(
            xp, mod_p, norm_g[l], cos_p, sin_p, w16, 1, PROJ_TOKENS, True)
        oa = _diff_attn(lamv, qa, ka, va, subln_g[l], lam_init)
        ob = _sb_attn(qb, kb, vb)
        xp = _merge(xp, mod_p, oa, ob, sza, szb, sga, sgb, wa16, wb16, wo16, final_g, 1, PROJ_TOKENS, final)
        for dst, val in zip(caches[:4], (ka32, va32, kb32, vb32)):
            dst.append(val)

        (ka32, va32, kb32, vb32, qa, ka, va, qb, kb, vb, sza, szb, sga, sgb) = _proj(
            xs, mod_s, norm_g[l], cos_s, sin_s, w16, bs, ns, False)
        oa = _diff_sample(lamv, qa, ka, va, cache_a_k[l].reshape(bs, past, WIDTH_A),
                          cache_a_v[l], subln_g[l], lam_init)
        ob = _sb_sample(qb, kb, vb, cache_b_k[l].reshape(bs, past, WIDTH_B),
                        cache_b_v[l].reshape(bs, past, WIDTH_B))
        xs = _merge(xs, mod_s, oa, ob, sza, szb, sga, sgb, wa16, wb16, wo16, final_g, bs, ns, final)
        for dst, val in zip(caches[4:], (ka32, va32, kb32, vb32)):
            dst.append(val)

    def stacked(vals, shape_tail):
        arr = vals[0][None] if depth == 1 else jnp.stack(vals)
        return arr.reshape(arr.shape[:3] + shape_tail)

    tails = [(N_HEADS_A, 2, HEAD_DIM_A), (N_HEADS_A, 2 * HEAD_DIM_A),
             (N_HEADS_B, HEAD_DIM_B), (N_HEADS_B, HEAD_DIM_B)] * 2
    return (xp, xs) + tuple(stacked(v, t) for v, t in zip(caches, tails))
```
